```python
import jax, jax.numpy as jnp
from jax import lax
import numpy as np

D_MODEL = 4096
BATCH = 4
SEQ = 2048
DEPTH = 2
DEC_BATCH = 8
DEC_SEQ = 4
PAST_LEN = 16384
PAGE_SIZE = 128

N_A = DEPTH // 2
N_B = DEPTH - N_A
N_HEADS = 32
HEAD_DIM = D_MODEL // N_HEADS
D_FF = ((8 * D_MODEL // 3 + 255) // 256) * 256
CONV_WIDTH = 31
Q_BLOCK = 128
N_SUB = 3
FFN_RES_W = 0.5
EPS = 1e-6
SB_BIAS_INIT = -6.0

kernel_name = "yoco_conformer_stickbreaking_decoder_step"


def rmsnorm(x, g):
    xf = x.astype(jnp.float32)
    y = xf * lax.rsqrt(jnp.mean(xf * xf, axis=-1, keepdims=True) + EPS)
    return (y * g.astype(jnp.float32)).astype(x.dtype)


def layernorm(x, g, b):
    xf = x.astype(jnp.float32)
    mu = jnp.mean(xf, axis=-1, keepdims=True)
    var = jnp.mean(jnp.square(xf - mu), axis=-1, keepdims=True)
    y = (xf - mu) * lax.rsqrt(var + EPS)
    return (y * g.astype(jnp.float32) + b.astype(jnp.float32)).astype(x.dtype)


def pre_norm(x, g, shift, scale):
    return rmsnorm(x, g) * (1 + scale[:, None, :]) + shift[:, None, :]


def post_norm_residual(x, y, g, gate, res_w):
    return x + res_w * gate[:, None, :] * rmsnorm(y, g)


def swiglu(h, w_gate, w_up, w_down):
    return (jax.nn.silu(h @ w_gate) * (h @ w_up)) @ w_down


def conv_module(h, conv_prev, w_pw1, b_pw1, w_dw, b_dw, ln_g, ln_b, w_pw2, b_pw2):
    a, g = jnp.split(h @ w_pw1 + b_pw1, 2, axis=-1)
    u = a * jax.nn.sigmoid(g)
    u_pad = jnp.concatenate([conv_prev.astype(u.dtype), u], axis=1)
    d = lax.conv_general_dilated(
        u_pad, w_dw[:, None, :].astype(u.dtype), window_strides=(1,), padding='VALID',
        dimension_numbers=('NWC', 'WIO', 'NWC'), feature_group_count=D_MODEL) + b_dw
    d = jax.nn.silu(layernorm(d, ln_g, ln_b))
    return d @ w_pw2 + b_pw2, u_pad[:, -(CONV_WIDTH - 1):]


def stick_breaking_block(q, q_pos, k, v, k_pos, bias):
    z = jnp.einsum('bqhd,bkhd->bhqk', q, k,
                   preferred_element_type=jnp.float32) * (HEAD_DIM ** -0.5)
    z = z + bias.astype(jnp.float32)[None, :, None, None]
    mask = k_pos[None, :] < q_pos[:, None]
    log_beta = jax.nn.log_sigmoid(z)
    log_1m = jnp.where(mask, log_beta - z, 0.0)
    incl = lax.cumsum(log_1m, axis=3, reverse=True)
    suffix = jnp.concatenate([incl[..., 1:], jnp.zeros_like(incl[..., :1])], axis=-1)
    att = jnp.where(mask, jnp.exp(log_beta + suffix), 0.0)
    out = jnp.einsum('bhqk,bkhd->bqhd', att.astype(v.dtype), v,
                     preferred_element_type=jnp.float32)
    return out.astype(v.dtype)


def stick_breaking_prompt(q, k, v, bias):
    B, T, H, Dh = q.shape
    nb = T // Q_BLOCK
    qb = q.reshape(B, nb, Q_BLOCK, H, Dh).transpose(1, 0, 2, 3, 4)
    pos = jnp.arange(T, dtype=jnp.int32)
    pb = pos.reshape(nb, Q_BLOCK)
    out = lax.map(lambda a: stick_breaking_block(a[0], a[1], k, v, pos, bias), (qb, pb))
    return out.transpose(1, 0, 2, 3, 4).reshape(B, T, H, Dh)


def trunk(x, c, conv_prev, k_past, v_past,
          w_mod, b_mod, norm_g, ffn_w_gate, ffn_w_up, ffn_w_down,
          conv_w_pw1, conv_b_pw1, conv_w_dw, conv_b_dw, conv_ln_g, conv_ln_b, conv_w_pw2, conv_b_pw2,
          kv_w_mod, kv_b_mod, kv_norm_g, w_kv, w_q, w_o, sb_bias):
    B, T, _ = x.shape
    sc = jax.nn.silu(c)
    conv_states = []
    k_new = v_new = k_all = v_all = None
    for l in range(DEPTH):
        if l == N_A:
            kv_mod = (sc @ kv_w_mod + kv_b_mod).reshape(B, 2, D_MODEL)
            h_kv = pre_norm(x, kv_norm_g, kv_mod[:, 0], kv_mod[:, 1])
            kv = (h_kv @ w_kv).reshape(B, T, 2, N_HEADS, HEAD_DIM)
            k_new, v_new = kv[:, :, 0], kv[:, :, 1]
            if k_past is None:
                k_all, v_all = k_new, v_new
            else:
                k_all = jnp.concatenate([k_past.astype(k_new.dtype), k_new], axis=1)
                v_all = jnp.concatenate([v_past.astype(v_new.dtype), v_new], axis=1)
        mod = (sc @ w_mod[l] + b_mod[l]).reshape(B, N_SUB, 3, D_MODEL)
        h = pre_norm(x, norm_g[l, 0, 0], mod[:, 0, 0], mod[:, 0, 1])
        y = swiglu(h, ffn_w_gate[l, 0], ffn_w_up[l, 0], ffn_w_down[l, 0])
        x = post_norm_residual(x, y, norm_g[l, 0, 1], mod[:, 0, 2], FFN_RES_W)
        h = pre_norm(x, norm_g[l, 1, 0], mod[:, 1, 0], mod[:, 1, 1])
        if l < N_A:
            y, st = conv_module(h, conv_prev[l], conv_w_pw1[l], conv_b_pw1[l], conv_w_dw[l], conv_b_dw[l],
                                conv_ln_g[l], conv_ln_b[l], conv_w_pw2[l], conv_b_pw2[l])
            conv_states.append(st)
        else:
            j = l - N_A
            q = (h @ w_q[j]).reshape(B, T, N_HEADS, HEAD_DIM)
            if k_past is None:
                o = stick_breaking_prompt(q, k_all, v_all, sb_bias[j])
            else:
                P = k_past.shape[1]
                q_pos = P + jnp.arange(T, dtype=jnp.int32)
                k_pos = jnp.arange(P + T, dtype=jnp.int32)
                o = stick_breaking_block(q, q_pos, k_all, v_all, k_pos, sb_bias[j])
            y = o.reshape(B, T, D_MODEL) @ w_o[j]
        x = post_norm_residual(x, y, norm_g[l, 1, 1], mod[:, 1, 2], 1.0)
        h = pre_norm(x, norm_g[l, 2, 0], mod[:, 2, 0], mod[:, 2, 1])
        y = swiglu(h, ffn_w_gate[l, 1], ffn_w_up[l, 1], ffn_w_down[l, 1])
        x = post_norm_residual(x, y, norm_g[l, 2, 1], mod[:, 2, 2], FFN_RES_W)
    return x, jnp.stack(conv_states, axis=0), k_new, v_new


def setup_inputs(seed: int = 0) -> dict:
    key = jax.random.key(seed)
    ks = jax.random.split(key, 40)
    f32 = jnp.float32
    D, F, W = D_MODEL, D_FF, CONV_WIDTH
    N_PAGES = PAST_LEN // PAGE_SIZE
    N_POOL = (DEC_BATCH * N_PAGES * 5) // 4

    def nrm(k, shape, s):
        return jax.random.normal(k, shape, f32) * s

    page_table = jax.random.permutation(ks[7], N_POOL)[:DEC_BATCH * N_PAGES]
    page_table = page_table.reshape(DEC_BATCH, N_PAGES).astype(jnp.int32)
    return {
        "x_prompt": nrm(ks[0], (BATCH, SEQ, D), 1.0),
        "x_sample": nrm(ks[1], (DEC_BATCH, DEC_SEQ, D), 1.0),
        "c_prompt": nrm(ks[2], (BATCH, D), 1.0),
        "c_sample": nrm(ks[3], (DEC_BATCH, D), 1.0),
        "state_conv": nrm(ks[4], (N_A, DEC_BATCH, W - 1, D), 0.5),
        "cache_k": nrm(ks[5], (N_POOL, PAGE_SIZE, N_HEADS, HEAD_DIM), 1.0),
        "cache_v": nrm(ks[6], (N_POOL, PAGE_SIZE, N_HEADS, HEAD_DIM), 1.0),
        "page_table": page_table,
        "w_mod": nrm(ks[8], (DEPTH, D, N_SUB * 3 * D), 0.5 * D ** -0.5),
        "b_mod": nrm(ks[9], (DEPTH, N_SUB * 3 * D), 0.01),
        "norm_g": 1.0 + nrm(ks[10], (DEPTH, N_SUB, 2, D), 0.1),
        "ffn_w_gate": nrm(ks[11], (DEPTH, 2, D, F), D ** -0.5),
        "ffn_w_up": nrm(ks[12], (DEPTH, 2, D, F), D ** -0.5),
        "ffn_w_down": nrm(ks[13], (DEPTH, 2, F, D), F ** -0.5),
        "conv_w_pw1": nrm(ks[14], (N_A, D, 2 * D), D ** -0.5),
        "conv_b_pw1": nrm(ks[15], (N_A, 2 * D), 0.01),
        "conv_w_dw": nrm(ks[16], (N_A, W, D), W ** -0.5),
        "conv_b_dw": nrm(ks[17], (N_A, D), 0.01),
        "conv_ln_g": 1.0 + nrm(ks[18], (N_A, D), 0.1),
        "conv_ln_b": nrm(ks[19], (N_A, D), 0.01),
        "conv_w_pw2": nrm(ks[20], (N_A, D, D), D ** -0.5),
        "conv_b_pw2": nrm(ks[21], (N_A, D), 0.01),
        "kv_w_mod": nrm(ks[22], (D, 2 * D), 0.5 * D ** -0.5),
        "kv_b_mod": nrm(ks[23], (2 * D,), 0.01),
        "kv_norm_g": 1.0 + nrm(ks[24], (D,), 0.1),
        "w_kv": nrm(ks[25], (D, 2 * N_HEADS * HEAD_DIM), D ** -0.5),
        "w_q": nrm(ks[26], (N_B, D, N_HEADS * HEAD_DIM), D ** -0.5),
        "w_o": nrm(ks[27], (N_B, N_HEADS * HEAD_DIM, D), D ** -0.5),
        "sb_bias": SB_BIAS_INIT + nrm(ks[28], (N_B, N_HEADS), 0.1),
    }


def reference(x_prompt, x_sample, c_prompt, c_sample, state_conv, cache_k, cache_v, page_table,
              w_mod, b_mod, norm_g, ffn_w_gate, ffn_w_up, ffn_w_down,
              conv_w_pw1, conv_b_pw1, conv_w_dw, conv_b_dw, conv_ln_g, conv_ln_b, conv_w_pw2, conv_b_pw2,
              kv_w_mod, kv_b_mod, kv_norm_g, w_kv, w_q, w_o, sb_bias):
    weights = (w_mod, b_mod, norm_g, ffn_w_gate, ffn_w_up, ffn_w_down,
               conv_w_pw1, conv_b_pw1, conv_w_dw, conv_b_dw, conv_ln_g, conv_ln_b, conv_w_pw2, conv_b_pw2,
               kv_w_mod, kv_b_mod, kv_norm_g, w_kv, w_q, w_o, sb_bias)
    conv_zero = jnp.zeros((N_A, x_prompt.shape[0], CONV_WIDTH - 1, D_MODEL), x_prompt.dtype)
    y_prompt, conv_state_prompt, k_prompt, v_prompt = trunk(
        x_prompt, c_prompt, conv_zero, None, None, *weights)
    Bd, n_pages = page_table.shape
    k_past = cache_k[page_table].reshape(Bd, n_pages * cache_k.shape[1], N_HEADS, HEAD_DIM)
    v_past = cache_v[page_table].reshape(Bd, n_pages * cache_v.shape[1], N_HEADS, HEAD_DIM)
    y_sample, conv_state_sample, k_sample, v_sample = trunk(
        x_sample, c_sample, state_conv, k_past, v_past, *weights)
    return (y_prompt, y_sample, conv_state_prompt, conv_state_sample, k_prompt, v_prompt, k_sample, v_sample)
```

```python
import functools

import jax
import jax.numpy as jnp
from jax import lax
from jax.experimental import pallas as pl
from jax.experimental.pallas import tpu as pltpu

F32 = jnp.float32
BF16 = jnp.bfloat16

EPS = 1e-6
FFN_RES_W = 0.5
N_SUB = 3
V7X_VMEM_LIMIT_BYTES = 56 * 1024 * 1024
SUBLANES = 8
LANES = 128
CONV_HALO = 32


def _params(n_axes):
    return pltpu.CompilerParams(
        dimension_semantics=("arbitrary",) * n_axes,
        vmem_limit_bytes=V7X_VMEM_LIMIT_BYTES)


def _tile(dim, target, align):
    best = None
    t = align
    while t <= min(dim, target):
        if dim % t == 0:
            best = t
        t += align
    return best if best is not None else dim


def _mm_body(*refs, n_w, nk, has_bias, lhs_silu, combine):
    x_ref = refs[0]
    w_refs = refs[1:1 + n_w]
    n_b = n_w if has_bias else 0
    b_refs = refs[1 + n_w:1 + n_w + n_b]
    o_ref = refs[1 + n_w + n_b]
    acc_refs = refs[2 + n_w + n_b:]

    x = x_ref[...]
    if lhs_silu:
        xf = x.astype(F32)
        x = xf * jax.nn.sigmoid(xf)
    x = x.astype(BF16)
    parts = [jnp.dot(x, w[...].astype(BF16), preferred_element_type=F32) for w in w_refs]

    def finish(vals):
        if has_bias:
            vals = [v + b[...] for v, b in zip(vals, b_refs)]
        o_ref[...] = combine(*vals).astype(o_ref.dtype)

    if nk == 1:
        finish(parts)
        return

    k = pl.program_id(2)

    @pl.when(k == 0)
    def _():
        for a, p in zip(acc_refs, parts):
            a[...] = p

    @pl.when(k > 0)
    def _():
        for a, p in zip(acc_refs, parts):
            a[...] += p

    @pl.when(k == nk - 1)
    def _():
        finish([a[...] for a in acc_refs])


def _identity(v):
    return v


def _swiglu_combine(g, u):
    return (g * jax.nn.sigmoid(g)) * u


def _glu_combine(a, g):
    return a * jax.nn.sigmoid(g)


def _matmul(x, ws, n_out, *, biases=None, combine=_identity, out_dtype=F32,
            tm=1024, tn=512, tk=4096, lhs_silu=False):
    M, K = x.shape
    tm = _tile(M, tm, 16)
    tn = _tile(n_out, tn, LANES)
    tk = _tile(K, tk, LANES)
    nk = K // tk
    grid = (M // tm, n_out // tn, nk)
    n_w = len(ws)

    in_specs = [pl.BlockSpec((tm, tk), lambda i, j, k: (i, k))]
    args = [x]
    for w, lead, col0 in ws:
        assert col0 % tn == 0 and w.shape[-2] == K
        nlead = len(lead)
        in_specs.append(pl.BlockSpec(
            (None,) * nlead + (tk, tn),
            functools.partial(lambda i, j, k, lead, cb: lead + (k, j + cb), lead=tuple(lead), cb=col0 // tn)))
        args.append(w)
    if biases is not None:
        for b in biases:
            in_specs.append(pl.BlockSpec((1, tn), lambda i, j, k: (0, j)))
            args.append(b.reshape(1, n_out).astype(F32))
    scratch = [pltpu.VMEM((tm, tn), F32) for _ in range(n_w)] if nk > 1 else []
    body = functools.partial(_mm_body, n_w=n_w, nk=nk, has_bias=biases is not None,
                             lhs_silu=lhs_silu, combine=combine)
    return pl.pallas_call(
        body,
        grid=grid,
        in_specs=in_specs,
        out_specs=pl.BlockSpec((tm, tn), lambda i, j, k: (i, j)),
        out_shape=jax.ShapeDtypeStruct((M, n_out), out_dtype),
        scratch_shapes=scratch,
        compiler_params=_params(3),
    )(*args)


def _rms(v, g):
    return v * lax.rsqrt(jnp.mean(v * v, axis=-1, keepdims=True) + EPS) * g


def _prenorm_body(x_ref, g_ref, shift_ref, scale_ref, h_ref):
    y = _rms(x_ref[...], g_ref[...])
    h_ref[...] = (y * (1.0 + scale_ref[...]) + shift_ref[...]).astype(h_ref.dtype)


def _post_pre_body(*refs, res_w, n_next):
    x_ref, y_ref, gpost_ref, gate_ref = refs[:4]
    nxt = refs[4:4 + 3 * n_next]
    xo_ref = refs[4 + 3 * n_next]
    h_refs = refs[5 + 3 * n_next:]
    xn = x_ref[...] + (res_w * gate_ref[...]) * _rms(y_ref[...], gpost_ref[...])
    xo_ref[...] = xn
    for n in range(n_next):
        g_ref, shift_ref, scale_ref = nxt[3 * n:3 * n + 3]
        y = _rms(xn, g_ref[...])
        h_refs[n][...] = (y * (1.0 + scale_ref[...]) + shift_ref[...]).astype(h_refs[n].dtype)


def _seq_vec(vec, M, seq_len, tm):
    nseq, D = vec.shape
    if seq_len % tm == 0:
        per = seq_len // tm
        return vec.reshape(nseq, 1, D), pl.BlockSpec((None, 1, D), lambda i: (i // per, 0, 0))
    rows = jnp.broadcast_to(vec[:, None, :], (nseq, seq_len, D)).reshape(M, D)
    return rows, pl.BlockSpec((tm, D), lambda i: (i, 0))


def _gain(g):
    D = g.shape[-1]
    return g.reshape(1, D), pl.BlockSpec((1, D), lambda i: (0, 0))


def _prenorm(x, g, shift, scale, seq_len, tm=256):
    M, D = x.shape
    tm = _tile(M, tm, 16)
    row = pl.BlockSpec((tm, D), lambda i: (i, 0))
    ga, gs = _gain(g)
    sa, ss = _seq_vec(shift, M, seq_len, tm)
    ca, cs = _seq_vec(scale, M, seq_len, tm)
    return pl.pallas_call(
        _prenorm_body, grid=(M // tm,),
        in_specs=[row, gs, ss, cs], out_specs=row,
        out_shape=jax.ShapeDtypeStruct((M, D), BF16),
        compiler_params=_params(1),
    )(x, ga, sa, ca)


def _post_pre(x, y, g_post, gate, res_w, nexts, seq_len, tm=256):
    M, D = x.shape
    tm = _tile(M, tm, 16)
    row = pl.BlockSpec((tm, D), lambda i: (i, 0))
    ga, gs = _gain(g_post)
    ta, ts = _seq_vec(gate, M, seq_len, tm)
    args, specs = [x, y, ga, ta], [row, row, gs, ts]
    for g, shift, scale in nexts:
        a, s = _gain(g)
        args.append(a), specs.append(s)
        a, s = _seq_vec(shift, M, seq_len, tm)
        args.append(a), specs.append(s)
        a, s = _seq_vec(scale, M, seq_len, tm)
        args.append(a), specs.append(s)
    n_next = len(nexts)
    outs = pl.pallas_call(
        functools.partial(_post_pre_body, res_w=res_w, n_next=n_next),
        grid=(M // tm,), in_specs=specs,
        out_specs=[row] * (1 + n_next),
        out_shape=[jax.ShapeDtypeStruct((M, D), F32)] + [jax.ShapeDtypeStruct((M, D), BF16)] * n_next,
        compiler_params=_params(1),
    )(*args)
    return outs[0], list(outs[1:])


def _dwconv_body(prev_ref, u_ref, wdw_ref, bdw_ref, lng_ref, lnb_ref, d_ref, win_ref, sh_ref,
                 pre_ref, *, tt, width, rows_chunk, lanes_chunk):
    D = u_ref.shape[-1]
    i = pl.program_id(1)

    @pl.when(i == 0)
    def _():
        win_ref[0:CONV_HALO, :] = prev_ref[...]

    win_ref[CONV_HALO:CONV_HALO + tt, :] = u_ref[...]
    first = CONV_HALO - (width - 1)
    slab_rows = tt + CONV_HALO - SUBLANES

    for c in range(D // lanes_chunk):
        cs = slice(c * lanes_chunk, (c + 1) * lanes_chunk)
        for s in range(1, SUBLANES):
            sh_ref[s - 1, 0:slab_rows, :] = win_ref[s:s + slab_rows, cs]

        def row_body(r, carry, cs=cs):
            r0 = pl.multiple_of(r * rows_chunk, rows_chunk)
            acc = jnp.zeros((rows_chunk, lanes_chunk), F32)
            for w in range(width):
                a, s = divmod(first + w, SUBLANES)
                rows = pl.ds(r0 + a * SUBLANES, rows_chunk)
                tap = win_ref[rows, cs] if s == 0 else sh_ref[s - 1, rows, :]
                acc = acc + wdw_ref[w:w + 1, cs] * tap
            pre_ref[pl.ds(r0, rows_chunk), cs] = acc + bdw_ref[:, cs]
            return carry

        lax.fori_loop(0, tt // rows_chunk, row_body, 0)

    d = pre_ref[...]
    mu = jnp.mean(d, axis=-1, keepdims=True)
    var = jnp.mean(jnp.square(d - mu), axis=-1, keepdims=True)
    y = (d - mu) * lax.rsqrt(var + EPS) * lng_ref[...] + lnb_ref[...]
    d_ref[...] = (y * jax.nn.sigmoid(y)).astype(d_ref.dtype)

    @pl.when(i + 1 < pl.num_programs(1))
    def _():
        win_ref[0:CONV_HALO, :] = win_ref[tt:tt + CONV_HALO, :]


def _dwconv(u, prev, w_dw, b_dw, ln_g, ln_b, tt=256):
    B, T, D = u.shape
    width = w_dw.shape[0]
    assert width - 1 <= CONV_HALO
    tt = _tile(T, tt, SUBLANES)
    assert tt >= CONV_HALO or T == tt
    rows_chunk = _tile(tt, 16, SUBLANES)
    lanes_chunk = _tile(D, 512, LANES)
    vec = pl.BlockSpec((1, D), lambda b, i: (0, 0))
    body = functools.partial(_dwconv_body, tt=tt, width=width, rows_chunk=rows_chunk,
                             lanes_chunk=lanes_chunk)
    return pl.pallas_call(
        body, grid=(B, T // tt),
        in_specs=[pl.BlockSpec((None, CONV_HALO, D), lambda b, i: (b, 0, 0)),
                  pl.BlockSpec((None, tt, D), lambda b, i: (b, i, 0)),
                  pl.BlockSpec((width, D), lambda b, i: (0, 0)), vec, vec, vec],
        out_specs=pl.BlockSpec((None, tt, D), lambda b, i: (b, i, 0)),
        out_shape=jax.ShapeDtypeStruct((B, T, D), BF16),
        scratch_shapes=[pltpu.VMEM((CONV_HALO + tt, D), F32),
                        pltpu.VMEM((SUBLANES - 1, CONV_HALO + tt, lanes_chunk), F32),
                        pltpu.VMEM((tt, D), F32)],
        compiler_params=_params(2),
    )(prev, u, w_dw, b_dw.reshape(1, D), ln_g.reshape(1, D), ln_b.reshape(1, D))


def _log_sigmoid(z):
    return jnp.minimum(z, 0.0) - jnp.log1p(jnp.exp(-jnp.abs(z)))


def _split_bf16(v):
    hi = v.astype(BF16)
    lo = (v - hi.astype(F32)).astype(BF16)
    return hi, lo


def _sb_prompt_body(bias_ref, q_ref, k_ref, v_ref, o_ref, *, tq, scale):
    h = pl.program_id(1)
    qi = pl.program_id(2)
    bias = bias_ref[h]
    q = q_ref[...]
    dh = q.shape[-1]
    row = lax.broadcasted_iota(jnp.int32, (tq, tq), 0)
    col = lax.broadcasted_iota(jnp.int32, (tq, tq), 1)
    col_w = lax.broadcasted_iota(jnp.int32, (tq, tq + LANES), 1)
    row_w = lax.broadcasted_iota(jnp.int32, (tq, tq + LANES), 0)
    sums = jnp.where((col_w >= tq) | (row_w > col_w), 1.0, 0.0).astype(BF16)
    reps = tq // LANES

    def tile(kstart, carry, acc, diag):
        kt = k_ref[pl.ds(kstart, tq), :].astype(BF16)
        vt = v_ref[pl.ds(kstart, tq), :].astype(BF16)
        z = lax.dot_general(q, kt, (((1,), (1,)), ((), ())), preferred_element_type=F32) * scale + bias
        lb = _log_sigmoid(z)
        l1m = lb - z
        if diag:
            mask = col < row
            l1m = jnp.where(mask, l1m, 0.0)
        hi, lo = _split_bf16(l1m)
        st = (jnp.dot(hi, sums, preferred_element_type=F32)
              + jnp.dot(lo, sums, preferred_element_type=F32))
        suffix = st[:, :tq] + jnp.concatenate([carry] * reps, axis=1)
        att = jnp.exp(lb + suffix)
        if diag:
            att = jnp.where(mask, att, 0.0)
        acc = acc + jnp.dot(att.astype(BF16), vt, preferred_element_type=F32)
        carry = carry + st[:, tq:]
        return carry, acc

    carry0 = jnp.zeros((tq, LANES), F32)
    acc0 = jnp.zeros((tq, dh), F32)
    carry, acc = tile(pl.multiple_of(qi * tq, tq), carry0, acc0, True)

    def body(it, ca):
        kstart = pl.multiple_of((qi - 1 - it) * tq, tq)
        return tile(kstart, ca[0], ca[1], False)

    carry, acc = lax.fori_loop(0, qi, body, (carry, acc))
    o_ref[...] = acc.astype(o_ref.dtype)


def _sb_prompt(q, k, v, bias, n_seq, seq_len, n_heads, tq=256):
    M, D = q.shape
    dh = D // n_heads
    tq = _tile(seq_len, tq, LANES)
    nq = seq_len // tq
    body = functools.partial(_sb_prompt_body, tq=tq, scale=dh ** -0.5)
    kv_spec = pl.BlockSpec((seq_len, dh), lambda b, h, i: (b, h))
    return pl.pallas_call(
        body, grid=(n_seq, n_heads, nq),
        in_specs=[pl.BlockSpec(memory_space=pltpu.SMEM),
                  pl.BlockSpec((tq, dh), lambda b, h, i: (b * nq + i, h)),
                  kv_spec, kv_spec],
        out_specs=pl.BlockSpec((tq, dh), lambda b, h, i: (b * nq + i, h)),
        out_shape=jax.ShapeDtypeStruct((M, D), BF16),
        compiler_params=_params(3),
    )(bias.astype(F32), q, k, v)


def _sb_sample_body(pt_ref, bias_ref, q_ref, kn_ref, vn_ref, *refs, pages_per_step, n_heads,
                    n_new, scale):
    k_refs = refs[:pages_per_step]
    v_refs = refs[pages_per_step:2 * pages_per_step]
    o_ref = refs[2 * pages_per_step]
    carry_ref, acc_ref = refs[2 * pages_per_step + 1:]
    s = pl.program_id(1)
    n_steps = pl.num_programs(1)
    q = q_ref[...]
    nq, dh = q.shape
    bias = bias_ref[...]
    head_of_row = lax.broadcasted_iota(jnp.int32, (1, n_heads, nq), 1)
    head_of_col = lax.broadcasted_iota(jnp.int32, (1, n_heads, nq), 2) % n_heads
    own = head_of_row == head_of_col

    def scores(k2, n_keys):
        zf = lax.dot_general(k2.astype(BF16), q, (((1,), (1,)), ((), ())),
                             preferred_element_type=F32)
        zc = jnp.sum(jnp.where(own, zf.reshape(n_keys, n_heads, nq), 0.0), axis=1)
        return zc * scale + bias

    def weighted(att, v2, n_keys):
        a3 = jnp.where(own, att[:, None, :], 0.0).astype(BF16).reshape(n_keys * n_heads, nq)
        return lax.dot_general(a3, v2.astype(BF16), (((0,), (0,)), ((), ())),
                               preferred_element_type=F32)

    @pl.when(s == 0)
    def _():
        n_keys = kn_ref.shape[0] // n_heads
        z = scores(kn_ref[...], n_keys)
        lb = _log_sigmoid(z)
        key = lax.broadcasted_iota(jnp.int32, (n_keys, nq), 0)
        t_of_col = lax.broadcasted_iota(jnp.int32, (n_keys, nq), 1) // n_heads
        mask = key < t_of_col
        l1m = jnp.where(mask, lb - z, 0.0)
        suffix = jnp.zeros((n_keys, nq), F32)
        for j in range(1, n_keys):
            suffix = suffix + jnp.where(key < j, l1m[j:j + 1, :], 0.0)
        att = jnp.where(mask, jnp.exp(lb + suffix), 0.0)
        acc_ref[...] = weighted(att, vn_ref[...], n_keys)
        carry_ref[...] = jnp.broadcast_to(jnp.sum(l1m, axis=0, keepdims=True), carry_ref.shape)

    for p in range(pages_per_step):
        kp = k_refs[p]
        n_keys = kp.shape[0]
        k2 = kp[...].reshape(n_keys * n_heads, dh)
        v2 = v_refs[p][...].reshape(n_keys * n_heads, dh)
        z = scores(k2, n_keys)
        lb = _log_sigmoid(z)
        l1m = lb - z
        hi, lo = _split_bf16(l1m)
        r = lax.broadcasted_iota(jnp.int32, (n_keys, n_keys), 0)
        c = lax.broadcasted_iota(jnp.int32, (n_keys, n_keys), 1)
        later = jnp.where(c > r, 1.0, 0.0).astype(BF16)
        carry = carry_ref[0:1, :]
        suffix = (jnp.dot(later, hi, preferred_element_type=F32)
                  + jnp.dot(later, lo, preferred_element_type=F32)) + carry
        att = jnp.exp(lb + suffix)
        acc_ref[...] += weighted(att, v2, n_keys)
        carry_ref[...] = jnp.broadcast_to(carry + jnp.sum(l1m, axis=0, keepdims=True),
                                          carry_ref.shape)

    @pl.when(s == n_steps - 1)
    def _():
        o_ref[...] = acc_ref[...].astype(o_ref.dtype)


def _sb_sample(q, k_new, v_new, cache_k, cache_v, page_table, bias, n_seq, n_new, n_heads,
               pages_per_step=2):
    M, D = q.shape
    dh = D // n_heads
    n_pages = page_table.shape[1]
    page = cache_k.shape[1]
    pages_per_step = _tile(n_pages, pages_per_step, 1)
    n_steps = n_pages // pages_per_step
    nq = n_new * n_heads
    pad_keys = SUBLANES
    q2 = q.reshape(n_seq, nq, dh)

    def pad_new(a):
        a = a.reshape(n_seq, nq, dh)
        return jnp.concatenate([a, jnp.zeros((n_seq, (pad_keys - n_new) * n_heads, dh), a.dtype)], axis=1)

    bias_row = jnp.tile(bias.astype(F32), n_new).reshape(1, nq)

    def page_spec(p):
        def imap(b, s, pt):
            logical = n_pages - 1 - (s * pages_per_step + p)
            return (pt[b * n_pages + logical], 0, 0, 0)
        return pl.BlockSpec((None, page, n_heads, dh), imap)

    seq_spec = lambda rows: pl.BlockSpec((None, rows, dh), lambda b, s, pt: (b, 0, 0))
    body = functools.partial(_sb_sample_body, pages_per_step=pages_per_step, n_heads=n_heads,
                             n_new=n_new, scale=dh ** -0.5)
    grid_spec = pltpu.PrefetchScalarGridSpec(
        num_scalar_prefetch=1, grid=(n_seq, n_steps),
        in_specs=[pl.BlockSpec((1, nq), lambda b, s, pt: (0, 0)),
                  seq_spec(nq), seq_spec(pad_keys * n_heads), seq_spec(pad_keys * n_heads)]
                 + [page_spec(p) for p in range(pages_per_step)] * 2,
        out_specs=seq_spec(nq),
        scratch_shapes=[pltpu.VMEM((SUBLANES, nq), F32), pltpu.VMEM((nq, dh), F32)])
    o = pl.pallas_call(
        body, grid_spec=grid_spec,
        out_shape=jax.ShapeDtypeStruct((n_seq, nq, dh), BF16),
        compiler_params=_params(2),
    )(page_table.reshape(-1).astype(jnp.int32), bias_row, q2, pad_new(k_new), pad_new(v_new),
      *([cache_k] * pages_per_step), *([cache_v] * pages_per_step))
    return o.reshape(M, D)


def _trunk(x, mods, kv_mod, conv_prev, past, weights, n_seq, seq_len):
    (norm_g, ffn_w_gate, ffn_w_up, ffn_w_down,
     conv_w_pw1, conv_b_pw1, conv_w_dw, conv_b_dw, conv_ln_g, conv_ln_b, conv_w_pw2, conv_b_pw2,
     kv_norm_g, w_kv, w_q, w_o, sb_bias) = weights
    M, D = x.shape
    depth = norm_g.shape[0]
    n_a = depth // 2
    n_heads = sb_bias.shape[1]
    d_ff = ffn_w_gate.shape[-1]

    def mod_of(l, sub, which):
        c0 = (sub * 3 + which) * D
        return mods[l][:, c0:c0 + D]

    def ffn(h, l, s):
        act = _matmul(h, [(ffn_w_gate, (l, s), 0), (ffn_w_up, (l, s), 0)], d_ff,
                      combine=_swiglu_combine, out_dtype=BF16, tm=1024, tn=256, tk=D)
        return _matmul(act, [(ffn_w_down, (l, s), 0)], D, tm=1024, tn=2048, tk=256)

    conv_states = []
    k_new = v_new = None
    h = _prenorm(x, norm_g[0, 0, 0], mod_of(0, 0, 0), mod_of(0, 0, 1), seq_len)
    h_kv = None
    for l in range(depth):
        if l == n_a:
            if h_kv is None:
                h_kv = _prenorm(x, kv_norm_g, kv_mod[:, :D], kv_mod[:, D:], seq_len)
            k_new = _matmul(h_kv, [(w_kv, (), 0)], D, tm=1024, tn=512, tk=D)
            v_new = _matmul(h_kv, [(w_kv, (), D)], D, tm=1024, tn=512, tk=D)
        y = ffn(h, l, 0)
        x, (h,) = _post_pre(x, y, norm_g[l, 0, 1], mod_of(l, 0, 2), FFN_RES_W,
                            [(norm_g[l, 1, 0], mod_of(l, 1, 0), mod_of(l, 1, 1))], seq_len)
        if l < n_a:
            u = _matmul(h, [(conv_w_pw1, (l,), 0), (conv_w_pw1, (l,), D)], D,
                        biases=[conv_b_pw1[l, :D], conv_b_pw1[l, D:]], combine=_glu_combine,
                        tm=1024, tn=256, tk=D)
            u3 = u.reshape(n_seq, seq_len, D)
            prev = conv_prev[l]
            keep = prev.shape[1]
            conv_states.append(jnp.concatenate([prev, u3], axis=1)[:, -keep:])
            prev_pad = jnp.concatenate(
                [jnp.zeros((n_seq, CONV_HALO - keep, D), F32), prev], axis=1)
            t_pad = -seq_len % SUBLANES
            u_in = u3 if t_pad == 0 else jnp.concatenate(
                [u3, jnp.zeros((n_seq, t_pad, D), F32)], axis=1)
            d = _dwconv(u_in, prev_pad, conv_w_dw[l], conv_b_dw[l], conv_ln_g[l], conv_ln_b[l])
            d = d[:, :seq_len].reshape(M, D)
            y = _matmul(d, [(conv_w_pw2, (l,), 0)], D, biases=[conv_b_pw2[l]], tm=1024, tn=512, tk=D)
        else:
            j = l - n_a
            q = _matmul(h, [(w_q, (j,), 0)], D, out_dtype=BF16, tm=1024, tn=512, tk=D)
            if past is None:
                o = _sb_prompt(q, k_new, v_new, sb_bias[j], n_seq, seq_len, n_heads)
            else:
                cache_k, cache_v, page_table = past
                o = _sb_sample(q, k_new, v_new, cache_k, cache_v, page_table, sb_bias[j],
                               n_seq, seq_len, n_heads)
            y = _matmul(o, [(w_o, (j,), 0)], D, tm=1024, tn=512, tk=D)
        x, (h,) = _post_pre(x, y, norm_g[l, 1, 1], mod_of(l, 1, 2), 1.0,
                            [(norm_g[l, 2, 0], mod_of(l, 2, 0), mod_of(l, 2, 1))], seq_len)
        y = ffn(h, l, 1)
        nexts = []
        if l + 1 < depth:
            nexts.append((norm_g[l + 1, 0, 0], mod_of(l + 1, 0, 0), mod_of(l + 1, 0, 1)))
            if l + 1 == n_a:
                nexts.append((kv_norm_g, kv_mod[:, :D], kv_mod[:, D:]))
        x, hs = _post_pre(x, y, norm_g[l, 2, 1], mod_of(l, 2, 2), FFN_RES_W, nexts, seq_len)
        if hs:
            h = hs[0]
            h_kv = hs[1] if len(hs) > 1 else None
    return x, jnp.stack(conv_states, axis=0), k_new, v_new


def kernel(x_prompt, x_sample, c_prompt, c_sample, state_conv, cache_k, cache_v, page_table, w_mod, b_mod, norm_g, ffn_w_gate, ffn_w_up, ffn_w_down, conv_w_pw1, conv_b_pw1, conv_w_dw, conv_b_dw, conv_ln_g, conv_ln_b, conv_w_pw2, conv_b_pw2, kv_w_mod, kv_b_mod, kv_norm_g, w_kv, w_q, w_o, sb_bias):
    B, T, D = x_prompt.shape
    Bd, Td, _ = x_sample.shape
    depth = w_mod.shape[0]
    n_heads = sb_bias.shape[1]
    dh = D // n_heads
    weights = (norm_g, ffn_w_gate, ffn_w_up, ffn_w_down,
               conv_w_pw1, conv_b_pw1, conv_w_dw, conv_b_dw, conv_ln_g, conv_ln_b, conv_w_pw2,
               conv_b_pw2, kv_norm_g, w_kv, w_q, w_o, sb_bias)

    n_c = B + Bd
    c_rows = -n_c % 16
    c_all = jnp.concatenate([c_prompt, c_sample, jnp.zeros((c_rows, D), F32)], axis=0)
    mods = [_matmul(c_all, [(w_mod, (l,), 0)], w_mod.shape[-1], biases=[b_mod[l]],
                    tm=16, tn=512, tk=D, lhs_silu=True) for l in range(depth)]
    kv_mod = _matmul(c_all, [(kv_w_mod, (), 0)], kv_w_mod.shape[-1], biases=[kv_b_mod],
                     tm=16, tn=512, tk=D, lhs_silu=True)

    conv_zero = jnp.zeros((depth // 2, B, conv_w_dw.shape[1] - 1, D), F32)
    y_p, cs_p, k_p, v_p = _trunk(
        x_prompt.reshape(B * T, D), [m[:B] for m in mods], kv_mod[:B], conv_zero, None,
        weights, B, T)
    y_s, cs_s, k_s, v_s = _trunk(
        x_sample.reshape(Bd * Td, D), [m[B:n_c] for m in mods], kv_mod[B:n_c], state_conv,
        (cache_k, cache_v, page_table), weights, Bd, Td)
    return (y_p.reshape(B, T, D), y_s.reshape(Bd, Td, D), cs_p, cs_s,
            k_p.reshape(B, T, n_heads, dh), v_p.reshape(B, T, n_heads, dh),
            k_s.reshape(Bd, Td, n_heads, dh), v_s.reshape(Bd, Td, n_heads, dh))
```

```python
import functools

import jax
import jax.numpy as jnp
from jax import lax
from jax.experimental import pallas as pl
from jax.experimental.pallas import tpu as pltpu

F32 = jnp.float32
BF16 = jnp.bfloat16

EPS = 1e-6
FFN_RES_W = 0.5
N_SUB = 3
V7X_VMEM_LIMIT_BYTES = 56 * 1024 * 1024
SUBLANES = 8
LANES = 128
CONV_HALO = 32
DOWN_TK = 1024


def _params(n_axes):
    return pltpu.CompilerParams(
        dimension_semantics=("arbitrary",) * n_axes,
        vmem_limit_bytes=V7X_VMEM_LIMIT_BYTES)


def _tile(dim, target, align):
    best = None
    t = align
    while t <= min(dim, target):
        if dim % t == 0:
            best = t
        t += align
    return best if best is not None else dim


def _mm_body(*refs, n_w, nk, has_bias, lhs_silu, combine):
    x_ref = refs[0]
    w_refs = refs[1:1 + n_w]
    n_b = n_w if has_bias else 0
    b_refs = refs[1 + n_w:1 + n_w + n_b]
    o_ref = refs[1 + n_w + n_b]
    acc_refs = refs[2 + n_w + n_b:]

    x = x_ref[...]
    if lhs_silu:
        xf = x.astype(F32)
        x = xf * jax.nn.sigmoid(xf)
    x = x.astype(BF16)
    parts = [jnp.dot(x, w[...].astype(BF16), preferred_element_type=F32) for w in w_refs]

    def finish(vals):
        if has_bias:
            vals = [v + b[...] for v, b in zip(vals, b_refs)]
        o_ref[...] = combine(*vals).astype(o_ref.dtype)

    if nk == 1:
        finish(parts)
        return

    k = pl.program_id(2)

    @pl.when(k == 0)
    def _():
        for a, p in zip(acc_refs, parts):
            a[...] = p

    @pl.when(k > 0)
    def _():
        for a, p in zip(acc_refs, parts):
            a[...] += p

    @pl.when(k == nk - 1)
    def _():
        finish([a[...] for a in acc_refs])


def _identity(v):
    return v


def _swiglu_combine(g, u):
    return (g * jax.nn.sigmoid(g)) * u


def _glu_combine(a, g):
    return a * jax.nn.sigmoid(g)


def _matmul(x, ws, n_out, *, biases=None, combine=_identity, out_dtype=F32,
            tm=1024, tn=512, tk=4096, lhs_silu=False):
    M, K = x.shape
    tm = _tile(M, tm, 16)
    tn = _tile(n_out, tn, LANES)
    tk = _tile(K, tk, LANES)
    nk = K // tk
    grid = (M // tm, n_out // tn, nk)
    n_w = len(ws)

    in_specs = [pl.BlockSpec((tm, tk), lambda i, j, k: (i, k))]
    args = [x]
    for w, lead, col0 in ws:
        assert col0 % tn == 0 and w.shape[-2] == K
        nlead = len(lead)
        in_specs.append(pl.BlockSpec(
            (None,) * nlead + (tk, tn),
            functools.partial(lambda i, j, k, lead, cb: lead + (k, j + cb), lead=tuple(lead), cb=col0 // tn)))
        args.append(w)
    if biases is not None:
        for b in biases:
            in_specs.append(pl.BlockSpec((1, tn), lambda i, j, k: (0, j)))
            args.append(b.reshape(1, n_out).astype(F32))
    scratch = [pltpu.VMEM((tm, tn), F32) for _ in range(n_w)] if nk > 1 else []
    body = functools.partial(_mm_body, n_w=n_w, nk=nk, has_bias=biases is not None,
                             lhs_silu=lhs_silu, combine=combine)
    return pl.pallas_call(
        body,
        grid=grid,
        in_specs=in_specs,
        out_specs=pl.BlockSpec((tm, tn), lambda i, j, k: (i, j)),
        out_shape=jax.ShapeDtypeStruct((M, n_out), out_dtype),
        scratch_shapes=scratch,
        compiler_params=_params(3),
    )(*args)


def _mm_ktail_body(x_ref, w_ref, o_ref, *, nk, tk, tail):
    k = pl.program_id(2)

    def part(kk):
        return jnp.dot(x_ref[:, :kk], w_ref[:kk, :].astype(BF16), preferred_element_type=F32)

    @pl.when(k == 0)
    def _():
        o_ref[...] = part(tk)

    if nk > 2:
        @pl.when((k > 0) & (k < nk - 1))
        def _():
            o_ref[...] += part(tk)

    @pl.when(k == nk - 1)
    def _():
        o_ref[...] += part(tail)


def _matmul_ktail(x, w, lead, n_out, *, tm, tn, tk):
    M, K = x.shape
    if K <= tk:
        return _matmul(x, [(w, lead, 0)], n_out, tm=tm, tn=tn, tk=K)
    tm = _tile(M, tm, 16)
    tn = _tile(n_out, tn, LANES)
    nk = pl.cdiv(K, tk)
    tail = K - (nk - 1) * tk
    assert tk % LANES == 0 and tail % LANES == 0 and w.shape[-2] == K
    lead = tuple(lead)
    body = functools.partial(_mm_ktail_body, nk=nk, tk=tk, tail=tail)
    return pl.pallas_call(
        body, grid=(M // tm, n_out // tn, nk),
        in_specs=[pl.BlockSpec((tm, tk), lambda i, j, k: (i, k)),
                  pl.BlockSpec((None,) * len(lead) + (tk, tn), lambda i, j, k: lead + (k, j))],
        out_specs=pl.BlockSpec((tm, tn), lambda i, j, k: (i, j)),
        out_shape=jax.ShapeDtypeStruct((M, n_out), F32),
        compiler_params=_params(3),
    )(x, w)


def _rms(v, g):
    return v * lax.rsqrt(jnp.mean(v * v, axis=-1, keepdims=True) + EPS) * g


def _prenorm_body(x_ref, g_ref, shift_ref, scale_ref, h_ref):
    y = _rms(x_ref[...], g_ref[...])
    h_ref[...] = (y * (1.0 + scale_ref[...]) + shift_ref[...]).astype(h_ref.dtype)


def _post_pre_body(*refs, res_w, n_next):
    x_ref, y_ref, gpost_ref, gate_ref = refs[:4]
    nxt = refs[4:4 + 3 * n_next]
    xo_ref = refs[4 + 3 * n_next]
    h_refs = refs[5 + 3 * n_next:]
    xn = x_ref[...] + (res_w * gate_ref[...]) * _rms(y_ref[...], gpost_ref[...])
    xo_ref[...] = xn
    for n in range(n_next):
        g_ref, shift_ref, scale_ref = nxt[3 * n:3 * n + 3]
        y = _rms(xn, g_ref[...])
        h_refs[n][...] = (y * (1.0 + scale_ref[...]) + shift_ref[...]).astype(h_refs[n].dtype)


def _seq_vec(vec, M, seq_len, tm):
    nseq, D = vec.shape
    if seq_len % tm == 0:
        per = seq_len // tm
        return vec.reshape(nseq, 1, D), pl.BlockSpec((None, 1, D), lambda i: (i // per, 0, 0))
    rows = jnp.broadcast_to(vec[:, None, :], (nseq, seq_len, D)).reshape(M, D)
    return rows, pl.BlockSpec((tm, D), lambda i: (i, 0))


def _gain(g):
    D = g.shape[-1]
    return g.reshape(1, D), pl.BlockSpec((1, D), lambda i: (0, 0))


def _prenorm(x, g, shift, scale, seq_len, tm=256):
    M, D = x.shape
    tm = _tile(M, tm, 16)
    row = pl.BlockSpec((tm, D), lambda i: (i, 0))
    ga, gs = _gain(g)
    sa, ss = _seq_vec(shift, M, seq_len, tm)
    ca, cs = _seq_vec(scale, M, seq_len, tm)
    return pl.pallas_call(
        _prenorm_body, grid=(M // tm,),
        in_specs=[row, gs, ss, cs], out_specs=row,
        out_shape=jax.ShapeDtypeStruct((M, D), BF16),
        compiler_params=_params(1),
    )(x, ga, sa, ca)


def _post_pre(x, y, g_post, gate, res_w, nexts, seq_len, tm=256):
    M, D = x.shape
    tm = _tile(M, tm, 16)
    row = pl.BlockSpec((tm, D), lambda i: (i, 0))
    ga, gs = _gain(g_post)
    ta, ts = _seq_vec(gate, M, seq_len, tm)
    args, specs = [x, y, ga, ta], [row, row, gs, ts]
    for g, shift, scale in nexts:
        a, s = _gain(g)
        args.append(a), specs.append(s)
        a, s = _seq_vec(shift, M, seq_len, tm)
        args.append(a), specs.append(s)
        a, s = _seq_vec(scale, M, seq_len, tm)
        args.append(a), specs.append(s)
    n_next = len(nexts)
    outs = pl.pallas_call(
        functools.partial(_post_pre_body, res_w=res_w, n_next=n_next),
        grid=(M // tm,), in_specs=specs,
        out_specs=[row] * (1 + n_next),
        out_shape=[jax.ShapeDtypeStruct((M, D), F32)] + [jax.ShapeDtypeStruct((M, D), BF16)] * n_next,
        compiler_params=_params(1),
    )(*args)
    return outs[0], list(outs[1:])


def _dwconv_body(prev_ref, u_ref, wdw_ref, bdw_ref, lng_ref, lnb_ref, d_ref, win_ref, sh_ref,
                 pre_ref, *, tt, width, rows_chunk, lanes_chunk):
    D = u_ref.shape[-1]
    i = pl.program_id(1)

    @pl.when(i == 0)
    def _():
        win_ref[0:CONV_HALO, :] = prev_ref[...]

    win_ref[CONV_HALO:CONV_HALO + tt, :] = u_ref[...]
    first = CONV_HALO - (width - 1)
    slab_rows = tt + CONV_HALO - SUBLANES

    for c in range(D // lanes_chunk):
        cs = slice(c * lanes_chunk, (c + 1) * lanes_chunk)
        for s in range(1, SUBLANES):
            sh_ref[s - 1, 0:slab_rows, :] = win_ref[s:s + slab_rows, cs]

        def row_body(r, carry, cs=cs):
            r0 = pl.multiple_of(r * rows_chunk, rows_chunk)
            acc = jnp.zeros((rows_chunk, lanes_chunk), F32)
            for w in range(width):
                a, s = divmod(first + w, SUBLANES)
                rows = pl.ds(r0 + a * SUBLANES, rows_chunk)
                tap = win_ref[rows, cs] if s == 0 else sh_ref[s - 1, rows, :]
                acc = acc + wdw_ref[w:w + 1, cs] * tap
            pre_ref[pl.ds(r0, rows_chunk), cs] = acc + bdw_ref[:, cs]
            return carry

        lax.fori_loop(0, tt // rows_chunk, row_body, 0)

    d = pre_ref[...]
    mu = jnp.mean(d, axis=-1, keepdims=True)
    var = jnp.mean(jnp.square(d - mu), axis=-1, keepdims=True)
    y = (d - mu) * lax.rsqrt(var + EPS) * lng_ref[...] + lnb_ref[...]
    d_ref[...] = (y * jax.nn.sigmoid(y)).astype(d_ref.dtype)

    @pl.when(i + 1 < pl.num_programs(1))
    def _():
        win_ref[0:CONV_HALO, :] = win_ref[tt:tt + CONV_HALO, :]


def _dwconv(u, prev, w_dw, b_dw, ln_g, ln_b, tt=256):
    B, T, D = u.shape
    width = w_dw.shape[0]
    assert width - 1 <= CONV_HALO
    tt = _tile(T, tt, SUBLANES)
    assert tt >= CONV_HALO or T == tt
    rows_chunk = _tile(tt, 16, SUBLANES)
    lanes_chunk = _tile(D, 512, LANES)
    vec = pl.BlockSpec((1, D), lambda b, i: (0, 0))
    body = functools.partial(_dwconv_body, tt=tt, width=width, rows_chunk=rows_chunk,
                             lanes_chunk=lanes_chunk)
    return pl.pallas_call(
        body, grid=(B, T // tt),
        in_specs=[pl.BlockSpec((None, CONV_HALO, D), lambda b, i: (b, 0, 0)),
                  pl.BlockSpec((None, tt, D), lambda b, i: (b, i, 0)),
                  pl.BlockSpec((width, D), lambda b, i: (0, 0)), vec, vec, vec],
        out_specs=pl.BlockSpec((None, tt, D), lambda b, i: (b, i, 0)),
        out_shape=jax.ShapeDtypeStruct((B, T, D), BF16),
        scratch_shapes=[pltpu.VMEM((CONV_HALO + tt, D), F32),
                        pltpu.VMEM((SUBLANES - 1, CONV_HALO + tt, lanes_chunk), F32),
                        pltpu.VMEM((tt, D), F32)],
        compiler_params=_params(2),
    )(prev, u, w_dw, b_dw.reshape(1, D), ln_g.reshape(1, D), ln_b.reshape(1, D))


def _log_sigmoid(z):
    return jnp.minimum(z, 0.0) - jnp.log(1.0 + jnp.exp(-jnp.abs(z)))


def _split_bf16(v):
    hi = v.astype(BF16)
    lo = (v - hi.astype(F32)).astype(BF16)
    return hi, lo


def _sb_prompt_body(bias_ref, q_ref, k_ref, v_ref, o_ref, *, tq, dh, heads, scale):
    hg = pl.program_id(1)
    qi = pl.program_id(2)
    row = lax.broadcasted_iota(jnp.int32, (tq, tq), 0)
    col = lax.broadcasted_iota(jnp.int32, (tq, tq), 1)
    visible = col < row
    later = jnp.where(row > col, 1.0, 0.0).astype(BF16)
    later2 = jnp.concatenate([later, later], axis=0)

    def scores(kblock, hh, diag):
        cs = slice(hh * dh, (hh + 1) * dh)
        kt = k_ref[pl.ds(pl.multiple_of(kblock * tq, tq), tq), cs].astype(BF16)
        z = lax.dot_general(q_ref[:, cs], kt, (((1,), (1,)), ((), ())),
                            preferred_element_type=F32) * scale + bias_ref[hg * heads + hh]
        lb = _log_sigmoid(z)
        l1m = lb - z
        if diag:
            l1m = jnp.where(visible, l1m, 0.0)
        hi, lo = _split_bf16(l1m)
        inner = jnp.dot(jnp.concatenate([hi, lo], axis=1), later2,
                        preferred_element_type=F32)
        return lb + inner, inner[:, 0:1] + l1m[:, 0:1]

    def attend(kblock, hh, sc, carry, acc, diag):
        cs = slice(hh * dh, (hh + 1) * dh)
        pre, total = sc
        vt = v_ref[pl.ds(pl.multiple_of(kblock * tq, tq), tq), cs].astype(BF16)
        att = jnp.exp(pre + carry)
        if diag:
            att = jnp.where(visible, att, 0.0)
        acc = acc + jnp.dot(att.astype(BF16), vt, preferred_element_type=F32)
        return total + carry, acc

    def store(accs):
        for hh in range(heads):
            o_ref[:, hh * dh:(hh + 1) * dh] = accs[hh].astype(o_ref.dtype)

    done, first = [], []
    for hh in range(heads):
        sc = scores(qi, hh, True)
        done.append(attend(qi, hh, sc, jnp.zeros((tq, 1), F32), jnp.zeros((tq, dh), F32), True))
        first.append(scores(jnp.maximum(qi - 1, 0), hh, False) + done[hh])
    first = tuple(first)

    @pl.when(qi == 0)
    def _():
        store([acc for _, acc in done])

    @pl.when(qi > 0)
    def _():

        def body(it, state):
            out = []
            for hh in range(heads):
                pre, total, carry, acc = state[hh]
                nxt = scores(qi - 1 - it, hh, False)
                out.append(nxt + attend(qi - it, hh, (pre, total), carry, acc, False))
            return tuple(out)

        state = lax.fori_loop(1, qi, body, first)
        accs = []
        for hh in range(heads):
            pre, total, carry, acc = state[hh]
            accs.append(attend(0, hh, (pre, total), carry, acc, False)[1])
        store(accs)


def _sb_prompt(q, k, v, bias, n_seq, seq_len, n_heads, tq=256, heads=2):
    M, D = q.shape
    dh = D // n_heads
    tq = _tile(seq_len, tq, LANES)
    nq = seq_len // tq
    heads = _tile(n_heads, heads, 1)
    wide = heads * dh
    body = functools.partial(_sb_prompt_body, tq=tq, dh=dh, heads=heads, scale=dh ** -0.5)
    kv_spec = pl.BlockSpec((seq_len, wide), lambda b, h, i: (b, h))
    return pl.pallas_call(
        body, grid=(n_seq, n_heads // heads, nq),
        in_specs=[pl.BlockSpec(memory_space=pltpu.SMEM),
                  pl.BlockSpec((tq, wide), lambda b, h, i: (b * nq + i, h)),
                  kv_spec, kv_spec],
        out_specs=pl.BlockSpec((tq, wide), lambda b, h, i: (b * nq + i, h)),
        out_shape=jax.ShapeDtypeStruct((M, D), BF16),
        compiler_params=_params(3),
    )(bias.astype(F32), q, k, v)


def _sb_sample_body(pt_ref, bias_ref, q_ref, kn_ref, vn_ref, *refs, pages_per_step, n_heads,
                    n_new, scale):
    k_refs = refs[:pages_per_step]
    v_refs = refs[pages_per_step:2 * pages_per_step]
    o_ref = refs[2 * pages_per_step]
    carry_ref, acc_ref = refs[2 * pages_per_step + 1:]
    s = pl.program_id(1)
    n_steps = pl.num_programs(1)
    q = q_ref[...]
    nq, dh = q.shape
    bias = bias_ref[...]
    head_of_row = lax.broadcasted_iota(jnp.int32, (1, n_heads, nq), 1)
    head_of_col = lax.broadcasted_iota(jnp.int32, (1, n_heads, nq), 2) % n_heads
    own = head_of_row == head_of_col

    def scores(k2, n_keys):
        zf = lax.dot_general(k2.astype(BF16), q, (((1,), (1,)), ((), ())),
                             preferred_element_type=F32)
        zc = jnp.sum(jnp.where(own, zf.reshape(n_keys, n_heads, nq), 0.0), axis=1)
        return zc * scale + bias

    def weighted(att, v2, n_keys):
        a3 = jnp.where(own, att[:, None, :], 0.0).astype(BF16).reshape(n_keys * n_heads, nq)
        return lax.dot_general(a3, v2.astype(BF16), (((0,), (0,)), ((), ())),
                               preferred_element_type=F32)

    @pl.when(s == 0)
    def _():
        n_keys = kn_ref.shape[0] // n_heads
        z = scores(kn_ref[...], n_keys)
        lb = _log_sigmoid(z)
        key = lax.broadcasted_iota(jnp.int32, (n_keys, nq), 0)
        t_of_col = lax.broadcasted_iota(jnp.int32, (n_keys, nq), 1) // n_heads
        mask = key < t_of_col
        l1m = jnp.where(mask, lb - z, 0.0)
        suffix = jnp.zeros((n_keys, nq), F32)
        for j in range(1, n_keys):
            suffix = suffix + jnp.where(key < j, l1m[j:j + 1, :], 0.0)
        att = jnp.where(mask, jnp.exp(lb + suffix), 0.0)
        acc_ref[...] = weighted(att, vn_ref[...], n_keys)
        carry_ref[...] = jnp.broadcast_to(jnp.sum(l1m, axis=0, keepdims=True), carry_ref.shape)

    for p in range(pages_per_step):
        kp = k_refs[p]
        n_keys = kp.shape[0]
        k2 = kp[...].reshape(n_keys * n_heads, dh)
        v2 = v_refs[p][...].reshape(n_keys * n_heads, dh)
        z = scores(k2, n_keys)
        lb = _log_sigmoid(z)
        l1m = lb - z
        hi, lo = _split_bf16(l1m)
        r = lax.broadcasted_iota(jnp.int32, (n_keys, n_keys), 0)
        c = lax.broadcasted_iota(jnp.int32, (n_keys, n_keys), 1)
        later = jnp.where(c > r, 1.0, 0.0).astype(BF16)
        carry = carry_ref[0:1, :]
        suffix = (jnp.dot(later, hi, preferred_element_type=F32)
                  + jnp.dot(later, lo, preferred_element_type=F32)) + carry
        att = jnp.exp(lb + suffix)
        acc_ref[...] += weighted(att, v2, n_keys)
        carry_ref[...] = jnp.broadcast_to(carry + jnp.sum(l1m, axis=0, keepdims=True),
                                          carry_ref.shape)

    @pl.when(s == n_steps - 1)
    def _():
        o_ref[...] = acc_ref[...].astype(o_ref.dtype)


def _sb_sample(q, k_new, v_new, cache_k, cache_v, page_table, bias, n_seq, n_new, n_heads,
               pages_per_step=2):
    M, D = q.shape
    dh = D // n_heads
    n_pages = page_table.shape[1]
    page = cache_k.shape[1]
    pages_per_step = _tile(n_pages, pages_per_step, 1)
    n_steps = n_pages // pages_per_step
    nq = n_new * n_heads
    pad_keys = SUBLANES
    q2 = q.reshape(n_seq, nq, dh)

    def pad_new(a):
        a = a.reshape(n_seq, nq, dh)
        return jnp.concatenate([a, jnp.zeros((n_seq, (pad_keys - n_new) * n_heads, dh), a.dtype)], axis=1)

    bias_row = jnp.tile(bias.astype(F32), n_new).reshape(1, nq)

    def page_spec(p):
        def imap(b, s, pt):
            logical = n_pages - 1 - (s * pages_per_step + p)
            return (pt[b * n_pages + logical], 0, 0, 0)
        return pl.BlockSpec((None, page, n_heads, dh), imap)

    seq_spec = lambda rows: pl.BlockSpec((None, rows, dh), lambda b, s, pt: (b, 0, 0))
    body = functools.partial(_sb_sample_body, pages_per_step=pages_per_step, n_heads=n_heads,
                             n_new=n_new, scale=dh ** -0.5)
    grid_spec = pltpu.PrefetchScalarGridSpec(
        num_scalar_prefetch=1, grid=(n_seq, n_steps),
        in_specs=[pl.BlockSpec((1, nq), lambda b, s, pt: (0, 0)),
                  seq_spec(nq), seq_spec(pad_keys * n_heads), seq_spec(pad_keys * n_heads)]
                 + [page_spec(p) for p in range(pages_per_step)] * 2,
        out_specs=seq_spec(nq),
        scratch_shapes=[pltpu.VMEM((SUBLANES, nq), F32), pltpu.VMEM((nq, dh), F32)])
    o = pl.pallas_call(
        body, grid_spec=grid_spec,
        out_shape=jax.ShapeDtypeStruct((n_seq, nq, dh), BF16),
        compiler_params=_params(2),
    )(page_table.reshape(-1).astype(jnp.int32), bias_row, q2, pad_new(k_new), pad_new(v_new),
      *([cache_k] * pages_per_step), *([cache_v] * pages_per_step))
    return o.reshape(M, D)


def _trunk(x, mods, kv_mod, conv_prev, past, weights, n_seq, seq_len):
    (norm_g, ffn_w_gate, ffn_w_up, ffn_w_down,
     conv_w_pw1, conv_b_pw1, conv_w_dw, conv_b_dw, conv_ln_g, conv_ln_b, conv_w_pw2, conv_b_pw2,
     kv_norm_g, w_kv, w_q, w_o, sb_bias) = weights
    M, D = x.shape
    depth = norm_g.shape[0]
    n_a = depth // 2
    n_heads = sb_bias.shape[1]
    d_ff = ffn_w_gate.shape[-1]

    def mod_of(l, sub, which):
        c0 = (sub * 3 + which) * D
        return mods[l][:, c0:c0 + D]

    def ffn(h, l, s):
        act = _matmul(h, [(ffn_w_gate, (l, s), 0), (ffn_w_up, (l, s), 0)], d_ff,
                      combine=_swiglu_combine, out_dtype=BF16, tm=1024, tn=256, tk=D)
        return _matmul_ktail(act, ffn_w_down, (l, s), D, tm=1024, tn=2048, tk=DOWN_TK)

    conv_states = []
    k_new = v_new = None
    h = _prenorm(x, norm_g[0, 0, 0], mod_of(0, 0, 0), mod_of(0, 0, 1), seq_len)
    h_kv = None
    for l in range(depth):
        if l == n_a:
            if h_kv is None:
                h_kv = _prenorm(x, kv_norm_g, kv_mod[:, :D], kv_mod[:, D:], seq_len)
            k_new = _matmul(h_kv, [(w_kv, (), 0)], D, tm=1024, tn=512, tk=D)
            v_new = _matmul(h_kv, [(w_kv, (), D)], D, tm=1024, tn=512, tk=D)
        y = ffn(h, l, 0)
        x, (h,) = _post_pre(x, y, norm_g[l, 0, 1], mod_of(l, 0, 2), FFN_RES_W,
                            [(norm_g[l, 1, 0], mod_of(l, 1, 0), mod_of(l, 1, 1))], seq_len)
        if l < n_a:
            u = _matmul(h, [(conv_w_pw1, (l,), 0), (conv_w_pw1, (l,), D)], D,
                        biases=[conv_b_pw1[l, :D], conv_b_pw1[l, D:]], combine=_glu_combine,
                        tm=1024, tn=256, tk=D)
            u3 = u.reshape(n_seq, seq_len, D)
            prev = conv_prev[l]
            keep = prev.shape[1]
            conv_states.append(jnp.concatenate([prev, u3], axis=1)[:, -keep:])
            prev_pad = jnp.concatenate(
                [jnp.zeros((n_seq, CONV_HALO - keep, D), F32), prev], axis=1)
            t_pad = -seq_len % SUBLANES
            u_in = u3 if t_pad == 0 else jnp.concatenate(
                [u3, jnp.zeros((n_seq, t_pad, D), F32)], axis=1)
            d = _dwconv(u_in, prev_pad, conv_w_dw[l], conv_b_dw[l], conv_ln_g[l], conv_ln_b[l])
            d = d[:, :seq_len].reshape(M, D)
            y = _matmul(d, [(conv_w_pw2, (l,), 0)], D, biases=[conv_b_pw2[l]], tm=1024, tn=512, tk=D)
        else:
            j = l - n_a
            q = _matmul(h, [(w_q, (j,), 0)], D, out_dtype=BF16, tm=1024, tn=512, tk=D)
            if past is None:
                o = _sb_prompt(q, k_new, v_new, sb_bias[j], n_seq, seq_len, n_heads)
            else:
                cache_k, cache_v, page_table = past
                o = _sb_sample(q, k_new, v_new, cache_k, cache_v, page_table, sb_bias[j],
                               n_seq, seq_len, n_heads)
            y = _matmul(o, [(w_o, (j,), 0)], D, tm=1024, tn=512, tk=D)
        x, (h,) = _post_pre(x, y, norm_g[l, 1, 1], mod_of(l, 1, 2), 1.0,
                            [(norm_g[l, 2, 0], mod_of(l, 2, 0), mod_of(l, 2, 1))], seq_len)
        y = ffn(h, l, 1)
        nexts = []
        if l + 1 < depth:
            nexts.append((norm_g[l + 1, 0, 0], mod_of(l + 1, 0, 0), mod_of(l + 1, 0, 1)))
            if l + 1 == n_a:
                nexts.append((kv_norm_g, kv_mod[:, :D], kv_mod[:, D:]))
        x, hs = _post_pre(x, y, norm_g[l, 2, 1], mod_of(l, 2, 2), FFN_RES_W, nexts, seq_len)
        if hs:
            h = hs[0]
            h_kv = hs[1] if len(hs) > 1 else None
    return x, jnp.stack(conv_states, axis=0), k_new, v_new


def kernel(x_prompt, x_sample, c_prompt, c_sample, state_conv, cache_k, cache_v, page_table, w_mod, b_mod, norm_g, ffn_w_gate, ffn_w_up, ffn_w_down, conv_w_pw1, conv_b_pw1, conv_w_dw, conv_b_dw, conv_ln_g, conv_ln_b, conv_w_pw2, conv_b_pw2, kv_w_mod, kv_b_mod, kv_norm_g, w_kv, w_q, w_o, sb_bias):
    B, T, D = x_prompt.shape
    Bd, Td, _ = x_sample.shape
    depth = w_mod.shape[0]
    n_heads = sb_bias.shape[1]
    dh = D // n_heads
    weights = (norm_g, ffn_w_gate, ffn_w_up, ffn_w_down,
               conv_w_pw1, conv_b_pw1, conv_w_dw, conv_b_dw, conv_ln_g, conv_ln_b, conv_w_pw2,
               conv_b_pw2, kv_norm_g, w_kv, w_q, w_o, sb_bias)

    n_c = B + Bd
    c_rows = -n_c % 16
    c_all = jnp.concatenate([c_prompt, c_sample, jnp.zeros((c_rows, D), F32)], axis=0)
    mods = [_matmul(c_all, [(w_mod, (l,), 0)], w_mod.shape[-1], biases=[b_mod[l]],
                    tm=16, tn=512, tk=D, lhs_silu=True) for l in range(depth)]
    kv_mod = _matmul(c_all, [(kv_w_mod, (), 0)], kv_w_mod.shape[-1], biases=[kv_b_mod],
                     tm=16, tn=512, tk=D, lhs_silu=True)

    conv_zero = jnp.zeros((depth // 2, B, conv_w_dw.shape[1] - 1, D), F32)
    y_p, cs_p, k_p, v_p = _trunk(
        x_prompt.reshape(B * T, D), [m[:B] for m in mods], kv_mod[:B], conv_zero, None,
        weights, B, T)
    y_s, cs_s, k_s, v_s = _trunk(
        x_sample.reshape(Bd * Td, D), [m[B:n_c] for m in mods], kv_mod[B:n_c], state_conv,
        (cache_k, cache_v, page_table), weights, Bd, Td)
    return (y_p.reshape(B, T, D), y_s.reshape(Bd, Td, D), cs_p, cs_s,
            k_p.reshape(B, T, n_heads, dh), v_p.reshape(B, T, n_heads, dh),
            k_s.reshape(Bd, Td, n_heads, dh), v_s.reshape(Bd, Td, n_heads, dh))
```

```python
import functools

import jax
import jax.numpy as jnp
from jax import lax
from jax.experimental import pallas as pl
from jax.experimental.pallas import tpu as pltpu

F32 = jnp.float32
BF16 = jnp.bfloat16

EPS = 1e-6
FFN_RES_W = 0.5
N_SUB = 3
V7X_VMEM_LIMIT_BYTES = 56 * 1024 * 1024
SUBLANES = 8
LANES = 128
CONV_HALO = 32
DOWN_TK = 1024


def _params(n_axes):
    return pltpu.CompilerParams(
        dimension_semantics=("arbitrary",) * n_axes,
        vmem_limit_bytes=V7X_VMEM_LIMIT_BYTES)


def _tile(dim, target, align):
    best = None
    t = align
    while t <= min(dim, target):
        if dim % t == 0:
            best = t
        t += align
    return best if best is not None else dim


def _mm_body(*refs, n_w, nk, has_bias, lhs_silu, combine, has_extra):
    n_x = 2 if has_extra else 1
    x_ref = refs[0]
    w_refs = refs[n_x:n_x + n_w]
    n_b = n_w if has_bias else 0
    b_refs = refs[n_x + n_w:n_x + n_w + n_b]
    o_ref = refs[n_x + n_w + n_b]
    acc_refs = refs[2 * n_x + n_w + n_b:]

    def lhs(ref):
        x = ref[...]
        if lhs_silu:
            xf = x.astype(F32)
            x = xf * jax.nn.sigmoid(xf)
        return x.astype(BF16)

    w_bf = [w[...].astype(BF16) for w in w_refs]
    x = lhs(x_ref)
    parts = [jnp.dot(x, w, preferred_element_type=F32) for w in w_bf]

    def finish(vals, out_ref=o_ref):
        if has_bias:
            vals = [v + b[...] for v, b in zip(vals, b_refs)]
        out_ref[...] = combine(*vals).astype(out_ref.dtype)

    if nk == 1:
        finish(parts)
        if has_extra:
            @pl.when(pl.program_id(1) == 0)
            def _():
                xe = lhs(refs[1])
                finish([jnp.dot(xe, w, preferred_element_type=F32) for w in w_bf],
                       refs[n_x + n_w + n_b + 1])
        return

    k = pl.program_id(2)

    @pl.when(k == 0)
    def _():
        for a, p in zip(acc_refs, parts):
            a[...] = p

    @pl.when(k > 0)
    def _():
        for a, p in zip(acc_refs, parts):
            a[...] += p

    @pl.when(k == nk - 1)
    def _():
        finish([a[...] for a in acc_refs])


def _identity(v):
    return v


def _swiglu_combine(g, u):
    return (g * jax.nn.sigmoid(g)) * u


def _glu_combine(a, g):
    return a * jax.nn.sigmoid(g)


def _matmul(x, ws, n_out, *, biases=None, combine=_identity, out_dtype=F32,
            tm=1024, tn=512, tk=4096, lhs_silu=False, extra=None):
    M, K = x.shape
    tm = _tile(M, tm, 16)
    tn = _tile(n_out, tn, LANES)
    tk = _tile(K, tk, LANES)
    nk = K // tk
    grid = (n_out // tn, M // tm, nk)
    n_w = len(ws)
    has_extra = extra is not None
    assert not has_extra or nk == 1

    in_specs = [pl.BlockSpec((tm, tk), lambda j, i, k: (i, k))]
    args = [x]
    if has_extra:
        in_specs.append(pl.BlockSpec((extra.shape[0], tk), lambda j, i, k: (0, k)))
        args.append(extra)
    for w, lead, col0 in ws:
        assert col0 % tn == 0 and w.shape[-2] == K
        nlead = len(lead)
        in_specs.append(pl.BlockSpec(
            (None,) * nlead + (tk, tn),
            functools.partial(lambda j, i, k, lead, cb: lead + (k, j + cb), lead=tuple(lead), cb=col0 // tn)))
        args.append(w)
    if biases is not None:
        for b in biases:
            in_specs.append(pl.BlockSpec((1, tn), lambda j, i, k: (0, j)))
            args.append(b.reshape(1, n_out).astype(F32))
    scratch = [pltpu.VMEM((tm, tn), F32) for _ in range(n_w)] if nk > 1 else []
    body = functools.partial(_mm_body, n_w=n_w, nk=nk, has_bias=biases is not None,
                             lhs_silu=lhs_silu, combine=combine, has_extra=has_extra)
    out_specs = [pl.BlockSpec((tm, tn), lambda j, i, k: (i, j))]
    out_shape = [jax.ShapeDtypeStruct((M, n_out), out_dtype)]
    if has_extra:
        out_specs.append(pl.BlockSpec((extra.shape[0], tn), lambda j, i, k: (0, j)))
        out_shape.append(jax.ShapeDtypeStruct((extra.shape[0], n_out), out_dtype))
    outs = pl.pallas_call(
        body,
        grid=grid,
        in_specs=in_specs,
        out_specs=out_specs,
        out_shape=out_shape,
        scratch_shapes=scratch,
        compiler_params=_params(3),
    )(*args)
    return tuple(outs) if has_extra else outs[0]


def _mm_ktail_body(*refs, nk, tk, tail, has_extra):
    n_x = 2 if has_extra else 1
    w_ref = refs[n_x]
    k = pl.program_id(2)

    def accumulate(x_ref, o_ref):
        def part(kk):
            return jnp.dot(x_ref[:, :kk], w_ref[:kk, :].astype(BF16), preferred_element_type=F32)

        @pl.when(k == 0)
        def _():
            o_ref[...] = part(tk)

        if nk > 2:
            @pl.when((k > 0) & (k < nk - 1))
            def _():
                o_ref[...] += part(tk)

        @pl.when(k == nk - 1)
        def _():
            o_ref[...] += part(tail)

    accumulate(refs[0], refs[n_x + 1])
    if has_extra:
        @pl.when(pl.program_id(1) == 0)
        def _():
            accumulate(refs[1], refs[n_x + 2])


def _matmul_ktail(x, w, lead, n_out, *, tm, tn, tk, extra=None):
    M, K = x.shape
    if K <= tk:
        return _matmul(x, [(w, lead, 0)], n_out, tm=tm, tn=tn, tk=K, extra=extra)
    tm = _tile(M, tm, 16)
    tn = _tile(n_out, tn, LANES)
    nk = pl.cdiv(K, tk)
    tail = K - (nk - 1) * tk
    assert tk % LANES == 0 and tail % LANES == 0 and w.shape[-2] == K
    lead = tuple(lead)
    has_extra = extra is not None
    body = functools.partial(_mm_ktail_body, nk=nk, tk=tk, tail=tail, has_extra=has_extra)
    in_specs = [pl.BlockSpec((tm, tk), lambda j, i, k: (i, k))]
    out_specs = [pl.BlockSpec((tm, tn), lambda j, i, k: (i, j))]
    out_shape = [jax.ShapeDtypeStruct((M, n_out), F32)]
    args = [x]
    if has_extra:
        me = extra.shape[0]
        in_specs.append(pl.BlockSpec((me, tk), lambda j, i, k: (0, k)))
        out_specs.append(pl.BlockSpec((me, tn), lambda j, i, k: (0, j)))
        out_shape.append(jax.ShapeDtypeStruct((me, n_out), F32))
        args.append(extra)
    in_specs.append(pl.BlockSpec((None,) * len(lead) + (tk, tn), lambda j, i, k: lead + (k, j)))
    args.append(w)
    outs = pl.pallas_call(
        body, grid=(n_out // tn, M // tm, nk),
        in_specs=in_specs, out_specs=out_specs, out_shape=out_shape,
        compiler_params=_params(3),
    )(*args)
    return tuple(outs) if has_extra else outs[0]


def _rms(v, g):
    return v * lax.rsqrt(jnp.mean(v * v, axis=-1, keepdims=True) + EPS) * g


def _prenorm_body(x_ref, g_ref, shift_ref, scale_ref, h_ref):
    y = _rms(x_ref[...], g_ref[...])
    h_ref[...] = (y * (1.0 + scale_ref[...]) + shift_ref[...]).astype(h_ref.dtype)


def _post_pre_body(*refs, res_w, n_next):
    x_ref, y_ref, gpost_ref, gate_ref = refs[:4]
    nxt = refs[4:4 + 3 * n_next]
    xo_ref = refs[4 + 3 * n_next]
    h_refs = refs[5 + 3 * n_next:]
    xn = x_ref[...] + (res_w * gate_ref[...]) * _rms(y_ref[...], gpost_ref[...])
    xo_ref[...] = xn
    for n in range(n_next):
        g_ref, shift_ref, scale_ref = nxt[3 * n:3 * n + 3]
        y = _rms(xn, g_ref[...])
        h_refs[n][...] = (y * (1.0 + scale_ref[...]) + shift_ref[...]).astype(h_refs[n].dtype)


def _seq_vec(vec, M, seq_len, tm):
    nseq, D = vec.shape
    if seq_len % tm == 0:
        per = seq_len // tm
        return vec.reshape(nseq, 1, D), pl.BlockSpec((None, 1, D), lambda i: (i // per, 0, 0))
    rows = jnp.broadcast_to(vec[:, None, :], (nseq, seq_len, D)).reshape(M, D)
    return rows, pl.BlockSpec((tm, D), lambda i: (i, 0))


def _gain(g):
    D = g.shape[-1]
    return g.reshape(1, D), pl.BlockSpec((1, D), lambda i: (0, 0))


def _prenorm(x, g, shift, scale, seq_len, tm=256):
    M, D = x.shape
    tm = _tile(M, tm, 16)
    row = pl.BlockSpec((tm, D), lambda i: (i, 0))
    ga, gs = _gain(g)
    sa, ss = _seq_vec(shift, M, seq_len, tm)
    ca, cs = _seq_vec(scale, M, seq_len, tm)
    return pl.pallas_call(
        _prenorm_body, grid=(M // tm,),
        in_specs=[row, gs, ss, cs], out_specs=row,
        out_shape=jax.ShapeDtypeStruct((M, D), BF16),
        compiler_params=_params(1),
    )(x, ga, sa, ca)


def _post_pre(x, y, g_post, gate, res_w, nexts, seq_len, tm=256):
    M, D = x.shape
    tm = _tile(M, tm, 16)
    row = pl.BlockSpec((tm, D), lambda i: (i, 0))
    ga, gs = _gain(g_post)
    ta, ts = _seq_vec(gate, M, seq_len, tm)
    args, specs = [x, y, ga, ta], [row, row, gs, ts]
    for g, shift, scale in nexts:
        a, s = _gain(g)
        args.append(a), specs.append(s)
        a, s = _seq_vec(shift, M, seq_len, tm)
        args.append(a), specs.append(s)
        a, s = _seq_vec(scale, M, seq_len, tm)
        args.append(a), specs.append(s)
    n_next = len(nexts)
    outs = pl.pallas_call(
        functools.partial(_post_pre_body, res_w=res_w, n_next=n_next),
        grid=(M // tm,), in_specs=specs,
        out_specs=[row] * (1 + n_next),
        out_shape=[jax.ShapeDtypeStruct((M, D), F32)] + [jax.ShapeDtypeStruct((M, D), BF16)] * n_next,
        compiler_params=_params(1),
    )(*args)
    return outs[0], list(outs[1:])


def _dwconv_body(prev_ref, u_ref, wdw_ref, bdw_ref, lng_ref, lnb_ref, d_ref, win_ref, sh_ref,
                 pre_ref, *, tt, width, rows_chunk, lanes_chunk):
    D = u_ref.shape[-1]
    i = pl.program_id(1)

    @pl.when(i == 0)
    def _():
        win_ref[0:CONV_HALO, :] = prev_ref[...]

    win_ref[CONV_HALO:CONV_HALO + tt, :] = u_ref[...]
    first = CONV_HALO - (width - 1)
    slab_rows = tt + CONV_HALO - SUBLANES

    for c in range(D // lanes_chunk):
        cs = slice(c * lanes_chunk, (c + 1) * lanes_chunk)
        for s in range(1, SUBLANES):
            sh_ref[s - 1, 0:slab_rows, :] = win_ref[s:s + slab_rows, cs]

        def row_body(r, carry, cs=cs):
            r0 = pl.multiple_of(r * rows_chunk, rows_chunk)
            acc = jnp.zeros((rows_chunk, lanes_chunk), F32)
            for w in range(width):
                a, s = divmod(first + w, SUBLANES)
                rows = pl.ds(r0 + a * SUBLANES, rows_chunk)
                tap = win_ref[rows, cs] if s == 0 else sh_ref[s - 1, rows, :]
                acc = acc + wdw_ref[w:w + 1, cs] * tap
            pre_ref[pl.ds(r0, rows_chunk), cs] = acc + bdw_ref[:, cs]
            return carry

        lax.fori_loop(0, tt // rows_chunk, row_body, 0)

    d = pre_ref[...]
    mu = jnp.mean(d, axis=-1, keepdims=True)
    var = jnp.mean(jnp.square(d - mu), axis=-1, keepdims=True)
    y = (d - mu) * lax.rsqrt(var + EPS) * lng_ref[...] + lnb_ref[...]
    d_ref[...] = (y * jax.nn.sigmoid(y)).astype(d_ref.dtype)

    @pl.when(i + 1 < pl.num_programs(1))
    def _():
        win_ref[0:CONV_HALO, :] = win_ref[tt:tt + CONV_HALO, :]


def _dwconv(u, prev, w_dw, b_dw, ln_g, ln_b, tt=256):
    B, T, D = u.shape
    width = w_dw.shape[0]
    assert width - 1 <= CONV_HALO
    tt = _tile(T, tt, SUBLANES)
    assert tt >= CONV_HALO or T == tt
    rows_chunk = _tile(tt, 32, SUBLANES)
    lanes_chunk = _tile(D, 512, LANES)
    vec = pl.BlockSpec((1, D), lambda b, i: (0, 0))
    body = functools.partial(_dwconv_body, tt=tt, width=width, rows_chunk=rows_chunk,
                             lanes_chunk=lanes_chunk)
    return pl.pallas_call(
        body, grid=(B, T // tt),
        in_specs=[pl.BlockSpec((None, CONV_HALO, D), lambda b, i: (b, 0, 0)),
                  pl.BlockSpec((None, tt, D), lambda b, i: (b, i, 0)),
                  pl.BlockSpec((width, D), lambda b, i: (0, 0)), vec, vec, vec],
        out_specs=pl.BlockSpec((None, tt, D), lambda b, i: (b, i, 0)),
        out_shape=jax.ShapeDtypeStruct((B, T, D), BF16),
        scratch_shapes=[pltpu.VMEM((CONV_HALO + tt, D), F32),
                        pltpu.VMEM((SUBLANES - 1, CONV_HALO + tt, lanes_chunk), F32),
                        pltpu.VMEM((tt, D), F32)],
        compiler_params=_params(2),
    )(prev, u, w_dw, b_dw.reshape(1, D), ln_g.reshape(1, D), ln_b.reshape(1, D))


def _log_sigmoid(z):
    return jnp.minimum(z, 0.0) - jnp.log(1.0 + jnp.exp(-jnp.abs(z)))


def _split_bf16(v):
    hi = v.astype(BF16)
    lo = (v - hi.astype(F32)).astype(BF16)
    return hi, lo


def _sb_prompt_body(bias_ref, q_ref, k_ref, v_ref, o_ref, *, tq, dh, heads, scale):
    hg = pl.program_id(1)
    qi = pl.program_id(2)
    row = lax.broadcasted_iota(jnp.int32, (tq, tq), 0)
    col = lax.broadcasted_iota(jnp.int32, (tq, tq), 1)
    visible = col < row
    later = jnp.where(row > col, 1.0, 0.0).astype(BF16)
    later2 = jnp.concatenate([later, later], axis=0)

    def scores(kblock, hh, diag):
        cs = slice(hh * dh, (hh + 1) * dh)
        kt = k_ref[pl.ds(pl.multiple_of(kblock * tq, tq), tq), cs].astype(BF16)
        z = lax.dot_general(q_ref[:, cs], kt, (((1,), (1,)), ((), ())),
                            preferred_element_type=F32) * scale + bias_ref[hg * heads + hh]
        lb = _log_sigmoid(z)
        l1m = lb - z
        if diag:
            l1m = jnp.where(visible, l1m, 0.0)
        hi, lo = _split_bf16(l1m)
        inner = jnp.dot(jnp.concatenate([hi, lo], axis=1), later2,
                        preferred_element_type=F32)
        return lb + inner, inner[:, 0:1] + l1m[:, 0:1]

    def attend(kblock, hh, sc, carry, acc, diag):
        cs = slice(hh * dh, (hh + 1) * dh)
        pre, total = sc
        vt = v_ref[pl.ds(pl.multiple_of(kblock * tq, tq), tq), cs].astype(BF16)
        att = jnp.exp(pre + carry)
        if diag:
            att = jnp.where(visible, att, 0.0)
        acc = acc + jnp.dot(att.astype(BF16), vt, preferred_element_type=F32)
        return total + carry, acc

    def store(accs):
        for hh in range(heads):
            o_ref[:, hh * dh:(hh + 1) * dh] = accs[hh].astype(o_ref.dtype)

    done, first = [], []
    for hh in range(heads):
        sc = scores(qi, hh, True)
        done.append(attend(qi, hh, sc, jnp.zeros((tq, 1), F32), jnp.zeros((tq, dh), F32), True))
        first.append(scores(jnp.maximum(qi - 1, 0), hh, False) + done[hh])
    first = tuple(first)

    @pl.when(qi == 0)
    def _():
        store([acc for _, acc in done])

    @pl.when(qi > 0)
    def _():

        def body(it, state):
            out = []
            for hh in range(heads):
                pre, total, carry, acc = state[hh]
                nxt = scores(qi - 1 - it, hh, False)
                out.append(nxt + attend(qi - it, hh, (pre, total), carry, acc, False))
            return tuple(out)

        state = lax.fori_loop(1, qi, body, first)
        accs = []
        for hh in range(heads):
            pre, total, carry, acc = state[hh]
            accs.append(attend(0, hh, (pre, total), carry, acc, False)[1])
        store(accs)


def _sb_prompt(q, k, v, bias, n_seq, seq_len, n_heads, tq=256, heads=4):
    M, D = q.shape
    dh = D // n_heads
    tq = _tile(seq_len, tq, LANES)
    nq = seq_len // tq
    heads = _tile(n_heads, heads, 1)
    wide = heads * dh
    body = functools.partial(_sb_prompt_body, tq=tq, dh=dh, heads=heads, scale=dh ** -0.5)
    kv_spec = pl.BlockSpec((seq_len, wide), lambda b, h, i: (b, h))
    return pl.pallas_call(
        body, grid=(n_seq, n_heads // heads, nq),
        in_specs=[pl.BlockSpec(memory_space=pltpu.SMEM),
                  pl.BlockSpec((tq, wide), lambda b, h, i: (b * nq + i, h)),
                  kv_spec, kv_spec],
        out_specs=pl.BlockSpec((tq, wide), lambda b, h, i: (b * nq + i, h)),
        out_shape=jax.ShapeDtypeStruct((M, D), BF16),
        compiler_params=_params(3),
    )(bias.astype(F32), q, k, v)


def _sb_sample_body(pt_ref, bias_ref, q_ref, kn_ref, vn_ref, *refs, pages_per_step, n_heads,
                    n_new, scale):
    k_refs = refs[:pages_per_step]
    v_refs = refs[pages_per_step:2 * pages_per_step]
    o_ref = refs[2 * pages_per_step]
    carry_ref, acc_ref = refs[2 * pages_per_step + 1:]
    s = pl.program_id(1)
    n_steps = pl.num_programs(1)
    q = q_ref[...]
    nq, dh = q.shape
    bias = bias_ref[...]
    head_of_row = lax.broadcasted_iota(jnp.int32, (1, n_heads, nq), 1)
    head_of_col = lax.broadcasted_iota(jnp.int32, (1, n_heads, nq), 2) % n_heads
    own = head_of_row == head_of_col

    def scores(k2, n_keys):
        zf = lax.dot_general(k2.astype(BF16), q, (((1,), (1,)), ((), ())),
                             preferred_element_type=F32)
        zc = jnp.sum(jnp.where(own, zf.reshape(n_keys, n_heads, nq), 0.0), axis=1)
        return zc * scale + bias

    def weighted(att, v2, n_keys):
        a3 = jnp.where(own, att[:, None, :], 0.0).astype(BF16).reshape(n_keys * n_heads, nq)
        return lax.dot_general(a3, v2.astype(BF16), (((0,), (0,)), ((), ())),
                               preferred_element_type=F32)

    @pl.when(s == 0)
    def _():
        n_keys = kn_ref.shape[0] // n_heads
        z = scores(kn_ref[...], n_keys)
        lb = _log_sigmoid(z)
        key = lax.broadcasted_iota(jnp.int32, (n_keys, nq), 0)
        t_of_col = lax.broadcasted_iota(jnp.int32, (n_keys, nq), 1) // n_heads
        mask = key < t_of_col
        l1m = jnp.where(mask, lb - z, 0.0)
        suffix = jnp.zeros((n_keys, nq), F32)
        for j in range(1, n_keys):
            suffix = suffix + jnp.where(key < j, l1m[j:j + 1, :], 0.0)
        att = jnp.where(mask, jnp.exp(lb + suffix), 0.0)
        acc_ref[...] = weighted(att, vn_ref[...], n_keys)
        carry_ref[...] = jnp.broadcast_to(jnp.sum(l1m, axis=0, keepdims=True), carry_ref.shape)

    for p in range(pages_per_step):
        kp = k_refs[p]
        n_keys = kp.shape[0]
        k2 = kp[...].reshape(n_keys * n_heads, dh)
        v2 = v_refs[p][...].reshape(n_keys * n_heads, dh)
        z = scores(k2, n_keys)
        lb = _log_sigmoid(z)
        l1m = lb - z
        hi, lo = _split_bf16(l1m)
        r = lax.broadcasted_iota(jnp.int32, (n_keys, n_keys), 0)
        c = lax.broadcasted_iota(jnp.int32, (n_keys, n_keys), 1)
        later = jnp.where(c > r, 1.0, 0.0).astype(BF16)
        carry = carry_ref[0:1, :]
        suffix = (jnp.dot(later, hi, preferred_element_type=F32)
                  + jnp.dot(later, lo, preferred_element_type=F32)) + carry
        att = jnp.exp(lb + suffix)
        acc_ref[...] += weighted(att, v2, n_keys)
        carry_ref[...] = jnp.broadcast_to(carry + jnp.sum(l1m, axis=0, keepdims=True),
                                          carry_ref.shape)

    @pl.when(s == n_steps - 1)
    def _():
        o_ref[...] = acc_ref[...].astype(o_ref.dtype)


def _sb_sample(q, k_new, v_new, cache_k, cache_v, page_table, bias, n_seq, n_new, n_heads,
               pages_per_step=4):
    M, D = q.shape
    dh = D // n_heads
    n_pages = page_table.shape[1]
    page = cache_k.shape[1]
    pages_per_step = _tile(n_pages, pages_per_step, 1)
    n_steps = n_pages // pages_per_step
    nq = n_new * n_heads
    pad_keys = SUBLANES
    q2 = q.reshape(n_seq, nq, dh)

    def pad_new(a):
        a = a.reshape(n_seq, nq, dh)
        return jnp.concatenate([a, jnp.zeros((n_seq, (pad_keys - n_new) * n_heads, dh), a.dtype)], axis=1)

    bias_row = jnp.tile(bias.astype(F32), n_new).reshape(1, nq)

    def page_spec(p):
        def imap(b, s, pt):
            logical = n_pages - 1 - (s * pages_per_step + p)
            return (pt[b * n_pages + logical], 0, 0, 0)
        return pl.BlockSpec((None, page, n_heads, dh), imap)

    seq_spec = lambda rows: pl.BlockSpec((None, rows, dh), lambda b, s, pt: (b, 0, 0))
    body = functools.partial(_sb_sample_body, pages_per_step=pages_per_step, n_heads=n_heads,
                             n_new=n_new, scale=dh ** -0.5)
    grid_spec = pltpu.PrefetchScalarGridSpec(
        num_scalar_prefetch=1, grid=(n_seq, n_steps),
        in_specs=[pl.BlockSpec((1, nq), lambda b, s, pt: (0, 0)),
                  seq_spec(nq), seq_spec(pad_keys * n_heads), seq_spec(pad_keys * n_heads)]
                 + [page_spec(p) for p in range(pages_per_step)] * 2,
        out_specs=seq_spec(nq),
        scratch_shapes=[pltpu.VMEM((SUBLANES, nq), F32), pltpu.VMEM((nq, dh), F32)])
    o = pl.pallas_call(
        body, grid_spec=grid_spec,
        out_shape=jax.ShapeDtypeStruct((n_seq, nq, dh), BF16),
        compiler_params=_params(2),
    )(page_table.reshape(-1).astype(jnp.int32), bias_row, q2, pad_new(k_new), pad_new(v_new),
      *([cache_k] * pages_per_step), *([cache_v] * pages_per_step))
    return o.reshape(M, D)


def _trunk(groups, weights):
    (norm_g, ffn_w_gate, ffn_w_up, ffn_w_down,
     conv_w_pw1, conv_b_pw1, conv_w_dw, conv_b_dw, conv_ln_g, conv_ln_b, conv_w_pw2, conv_b_pw2,
     kv_norm_g, w_kv, w_q, w_o, sb_bias) = weights
    big, small = groups
    D = big["x"].shape[1]
    depth = norm_g.shape[0]
    n_a = depth // 2
    n_heads = sb_bias.shape[1]
    d_ff = ffn_w_gate.shape[-1]

    def mod_of(g, l, sub, which):
        c0 = (sub * 3 + which) * D
        return g["mods"][l][:, c0:c0 + D]

    def kv_mod_of(g):
        return g["kv_mod"][:, :D], g["kv_mod"][:, D:]

    def mm(hs, ws, n_out, **kw):
        return list(_matmul(hs[0], ws, n_out, extra=hs[1], tk=D, **kw))

    def ffn(hs, l, s):
        acts = mm(hs, [(ffn_w_gate, (l, s), 0), (ffn_w_up, (l, s), 0)], d_ff,
                  combine=_swiglu_combine, out_dtype=BF16, tm=1024, tn=256)
        return list(_matmul_ktail(acts[0], ffn_w_down, (l, s), D, tm=1024, tn=2048, tk=DOWN_TK,
                                  extra=acts[1]))

    def post_pre(xs, ys, l, sub, res_w, nexts_of):
        res = [_post_pre(x, y, norm_g[l, sub, 1], mod_of(g, l, sub, 2), res_w, nexts_of(g),
                         g["seq_len"]) for g, x, y in zip(groups, xs, ys)]
        return [r[0] for r in res], [r[1] for r in res]

    def conv_mixer(g, u, l, states):
        n_seq, seq_len = g["n_seq"], g["seq_len"]
        u3 = u.reshape(n_seq, seq_len, D)
        prev = g["conv_prev"][l]
        keep = prev.shape[1]
        states.append(jnp.concatenate([prev, u3], axis=1)[:, -keep:])
        prev_pad = jnp.concatenate([jnp.zeros((n_seq, CONV_HALO - keep, D), F32), prev], axis=1)
        t_pad = -seq_len % SUBLANES
        u_in = u3 if t_pad == 0 else jnp.concatenate(
            [u3, jnp.zeros((n_seq, t_pad, D), F32)], axis=1)
        d = _dwconv(u_in, prev_pad, conv_w_dw[l], conv_b_dw[l], conv_ln_g[l], conv_ln_b[l])
        return d[:, :seq_len].reshape(n_seq * seq_len, D)

    def attention(g, q, k_new, v_new, j):
        if g["past"] is None:
            return _sb_prompt(q, k_new, v_new, sb_bias[j], g["n_seq"], g["seq_len"], n_heads)
        cache_k, cache_v, page_table = g["past"]
        return _sb_sample(q, k_new, v_new, cache_k, cache_v, page_table, sb_bias[j],
                          g["n_seq"], g["seq_len"], n_heads)

    xs = [g["x"] for g in groups]
    conv_states = [[] for _ in groups]
    k_news = v_news = None
    hs = [_prenorm(x, norm_g[0, 0, 0], mod_of(g, 0, 0, 0), mod_of(g, 0, 0, 1), g["seq_len"])
          for g, x in zip(groups, xs)]
    h_kvs = None
    for l in range(depth):
        if l == n_a:
            if h_kvs is None:
                h_kvs = [_prenorm(x, kv_norm_g, *kv_mod_of(g), g["seq_len"])
                         for g, x in zip(groups, xs)]
            k_news = mm(h_kvs, [(w_kv, (), 0)], D, tm=1024, tn=512)
            v_news = mm(h_kvs, [(w_kv, (), D)], D, tm=1024, tn=512)
        ys = ffn(hs, l, 0)
        xs, nxt = post_pre(xs, ys, l, 0, FFN_RES_W,
                           lambda g: [(norm_g[l, 1, 0], mod_of(g, l, 1, 0), mod_of(g, l, 1, 1))])
        hs = [n[0] for n in nxt]
        if l < n_a:
            us = mm(hs, [(conv_w_pw1, (l,), 0), (conv_w_pw1, (l,), D)], D,
                    biases=[conv_b_pw1[l, :D], conv_b_pw1[l, D:]], combine=_glu_combine,
                    tm=1024, tn=256)
            ds = [conv_mixer(g, u, l, st) for g, u, st in zip(groups, us, conv_states)]
            ys = mm(ds, [(conv_w_pw2, (l,), 0)], D, biases=[conv_b_pw2[l]], tm=1024, tn=512)
        else:
            j = l - n_a
            qs = mm(hs, [(w_q, (j,), 0)], D, out_dtype=BF16, tm=1024, tn=512)
            os_ = [attention(g, q, kn, vn, j) for g, q, kn, vn in zip(groups, qs, k_news, v_news)]
            ys = mm(os_, [(w_o, (j,), 0)], D, tm=1024, tn=512)
        xs, nxt = post_pre(xs, ys, l, 1, 1.0,
                           lambda g: [(norm_g[l, 2, 0], mod_of(g, l, 2, 0), mod_of(g, l, 2, 1))])
        hs = [n[0] for n in nxt]
        ys = ffn(hs, l, 1)

        def nexts_of(g):
            nexts = []
            if l + 1 < depth:
                nexts.append((norm_g[l + 1, 0, 0], mod_of(g, l + 1, 0, 0), mod_of(g, l + 1, 0, 1)))
                if l + 1 == n_a:
                    nexts.append((kv_norm_g,) + kv_mod_of(g))
            return nexts

        xs, nxt = post_pre(xs, ys, l, 2, FFN_RES_W, nexts_of)
        if nxt[0]:
            hs = [n[0] for n in nxt]
            h_kvs = [n[1] for n in nxt] if len(nxt[0]) > 1 else None
    return [(x, jnp.stack(st, axis=0), kn, vn)
            for x, st, kn, vn in zip(xs, conv_states, k_news, v_news)]


def kernel(x_prompt, x_sample, c_prompt, c_sample, state_conv, cache_k, cache_v, page_table, w_mod, b_mod, norm_g, ffn_w_gate, ffn_w_up, ffn_w_down, conv_w_pw1, conv_b_pw1, conv_w_dw, conv_b_dw, conv_ln_g, conv_ln_b, conv_w_pw2, conv_b_pw2, kv_w_mod, kv_b_mod, kv_norm_g, w_kv, w_q, w_o, sb_bias):
    B, T, D = x_prompt.shape
    Bd, Td, _ = x_sample.shape
    depth = w_mod.shape[0]
    n_heads = sb_bias.shape[1]
    dh = D // n_heads
    weights = (norm_g, ffn_w_gate, ffn_w_up, ffn_w_down,
               conv_w_pw1, conv_b_pw1, conv_w_dw, conv_b_dw, conv_ln_g, conv_ln_b, conv_w_pw2,
               conv_b_pw2, kv_norm_g, w_kv, w_q, w_o, sb_bias)

    n_c = B + Bd
    c_rows = -n_c % 16
    c_all = jnp.concatenate([c_prompt, c_sample, jnp.zeros((c_rows, D), F32)], axis=0)
    mods = [_matmul(c_all, [(w_mod, (l,), 0)], w_mod.shape[-1], biases=[b_mod[l]],
                    tm=16, tn=512, tk=D, lhs_silu=True) for l in range(depth)]
    kv_mod = _matmul(c_all, [(kv_w_mod, (), 0)], kv_w_mod.shape[-1], biases=[kv_b_mod],
                     tm=16, tn=512, tk=D, lhs_silu=True)

    conv_zero = jnp.zeros((depth // 2, B, conv_w_dw.shape[1] - 1, D), F32)
    prompt = dict(x=x_prompt.reshape(B * T, D), mods=[m[:B] for m in mods], kv_mod=kv_mod[:B],
                  conv_prev=conv_zero, past=None, n_seq=B, seq_len=T)
    sample = dict(x=x_sample.reshape(Bd * Td, D), mods=[m[B:n_c] for m in mods],
                  kv_mod=kv_mod[B:n_c], conv_prev=state_conv,
                  past=(cache_k, cache_v, page_table), n_seq=Bd, seq_len=Td)
    (y_p, cs_p, k_p, v_p), (y_s, cs_s, k_s, v_s) = _trunk([prompt, sample], weights)
    return (y_p.reshape(B, T, D), y_s.reshape(Bd, Td, D), cs_p, cs_s,
            k_p.reshape(B, T, n_heads, dh), v_p.reshape(B, T, n_heads, dh),
            k_s.reshape(Bd, Td, n_heads, dh), v_s.reshape(Bd, Td, n_heads, dh))
```

```python
import functools

import jax
import jax.numpy as jnp
from jax import lax
from jax.experimental import pallas as pl
from jax.experimental.pallas import tpu as pltpu

F32 = jnp.float32
BF16 = jnp.bfloat16

EPS = 1e-6
FFN_RES_W = 0.5
N_SUB = 3
V7X_VMEM_LIMIT_BYTES = 56 * 1024 * 1024
SUBLANES = 8
LANES = 128
CONV_HALO = 32
DOWN_TK = 1024


def _params(n_axes):
    return pltpu.CompilerParams(
        dimension_semantics=("arbitrary",) * n_axes,
        vmem_limit_bytes=V7X_VMEM_LIMIT_BYTES)


def _tile(dim, target, align):
    best = None
    t = align
    while t <= min(dim, target):
        if dim % t == 0:
            best = t
        t += align
    return best if best is not None else dim


def _mm_body(*refs, n_w, nk, has_bias, lhs_silu, combine, has_extra):
    n_x = 2 if has_extra else 1
    x_ref = refs[0]
    w_refs = refs[n_x:n_x + n_w]
    n_b = n_w if has_bias else 0
    b_refs = refs[n_x + n_w:n_x + n_w + n_b]
    o_ref = refs[n_x + n_w + n_b]
    acc_refs = refs[2 * n_x + n_w + n_b:]

    def lhs(ref):
        x = ref[...]
        if lhs_silu:
            xf = x.astype(F32)
            x = xf * jax.nn.sigmoid(xf)
        return x.astype(BF16)

    w_bf = [w[...].astype(BF16) for w in w_refs]
    x = lhs(x_ref)

    def finish(vals, out_ref=o_ref):
        if has_bias:
            vals = [v + b[...] for v, b in zip(vals, b_refs)]
        out_ref[...] = combine(*vals).astype(out_ref.dtype)

    if nk == 1 and has_extra:
        i = pl.program_id(1)
        tm = x.shape[0]

        @pl.when(i == 0)
        def _():
            both = jnp.concatenate([x, lhs(refs[1])], axis=0)
            vals = [jnp.dot(both, w, preferred_element_type=F32) for w in w_bf]
            finish([v[:tm] for v in vals])
            finish([v[tm:] for v in vals], refs[n_x + n_w + n_b + 1])

        @pl.when(i > 0)
        def _():
            finish([jnp.dot(x, w, preferred_element_type=F32) for w in w_bf])
        return

    parts = [jnp.dot(x, w, preferred_element_type=F32) for w in w_bf]
    if nk == 1:
        finish(parts)
        return

    k = pl.program_id(2)

    @pl.when(k == 0)
    def _():
        for a, p in zip(acc_refs, parts):
            a[...] = p

    @pl.when(k > 0)
    def _():
        for a, p in zip(acc_refs, parts):
            a[...] += p

    @pl.when(k == nk - 1)
    def _():
        finish([a[...] for a in acc_refs])


def _identity(v):
    return v


def _swiglu_combine(g, u):
    return (g * jax.nn.sigmoid(g)) * u


def _glu_combine(a, g):
    return a * jax.nn.sigmoid(g)


def _matmul(x, ws, n_out, *, biases=None, combine=_identity, out_dtype=F32,
            tm=1024, tn=512, tk=4096, lhs_silu=False, extra=None):
    M, K = x.shape
    tm = _tile(M, tm, 16)
    tn = _tile(n_out, tn, LANES)
    tk = _tile(K, tk, LANES)
    nk = K // tk
    grid = (n_out // tn, M // tm, nk)
    n_w = len(ws)
    has_extra = extra is not None
    assert not has_extra or nk == 1

    in_specs = [pl.BlockSpec((tm, tk), lambda j, i, k: (i, k))]
    args = [x]
    if has_extra:
        in_specs.append(pl.BlockSpec((extra.shape[0], tk), lambda j, i, k: (0, k)))
        args.append(extra)
    for w, lead, col0 in ws:
        assert col0 % tn == 0 and w.shape[-2] == K
        nlead = len(lead)
        in_specs.append(pl.BlockSpec(
            (None,) * nlead + (tk, tn),
            functools.partial(lambda j, i, k, lead, cb: lead + (k, j + cb), lead=tuple(lead), cb=col0 // tn)))
        args.append(w)
    if biases is not None:
        for b in biases:
            in_specs.append(pl.BlockSpec((1, tn), lambda j, i, k: (0, j)))
            args.append(b.reshape(1, n_out).astype(F32))
    scratch = [pltpu.VMEM((tm, tn), F32) for _ in range(n_w)] if nk > 1 else []
    body = functools.partial(_mm_body, n_w=n_w, nk=nk, has_bias=biases is not None,
                             lhs_silu=lhs_silu, combine=combine, has_extra=has_extra)
    out_specs = [pl.BlockSpec((tm, tn), lambda j, i, k: (i, j))]
    out_shape = [jax.ShapeDtypeStruct((M, n_out), out_dtype)]
    if has_extra:
        out_specs.append(pl.BlockSpec((extra.shape[0], tn), lambda j, i, k: (0, j)))
        out_shape.append(jax.ShapeDtypeStruct((extra.shape[0], n_out), out_dtype))
    outs = pl.pallas_call(
        body,
        grid=grid,
        in_specs=in_specs,
        out_specs=out_specs,
        out_shape=out_shape,
        scratch_shapes=scratch,
        compiler_params=_params(3),
    )(*args)
    return tuple(outs) if has_extra else outs[0]


def _mm_ktail_body(*refs, nk, tk, tail, has_extra):
    n_x = 2 if has_extra else 1
    w_ref = refs[n_x]
    k = pl.program_id(2)

    def accumulate(x_ref, o_ref):
        def part(kk):
            return jnp.dot(x_ref[:, :kk], w_ref[:kk, :].astype(BF16), preferred_element_type=F32)

        @pl.when(k == 0)
        def _():
            o_ref[...] = part(tk)

        if nk > 2:
            @pl.when((k > 0) & (k < nk - 1))
            def _():
                o_ref[...] += part(tk)

        @pl.when(k == nk - 1)
        def _():
            o_ref[...] += part(tail)

    accumulate(refs[0], refs[n_x + 1])
    if has_extra:
        @pl.when(pl.program_id(1) == 0)
        def _():
            accumulate(refs[1], refs[n_x + 2])


def _matmul_ktail(x, w, lead, n_out, *, tm, tn, tk, extra=None):
    M, K = x.shape
    if K <= tk:
        return _matmul(x, [(w, lead, 0)], n_out, tm=tm, tn=tn, tk=K, extra=extra)
    tm = _tile(M, tm, 16)
    tn = _tile(n_out, tn, LANES)
    nk = pl.cdiv(K, tk)
    tail = K - (nk - 1) * tk
    assert tk % LANES == 0 and tail % LANES == 0 and w.shape[-2] == K
    lead = tuple(lead)
    has_extra = extra is not None
    body = functools.partial(_mm_ktail_body, nk=nk, tk=tk, tail=tail, has_extra=has_extra)
    in_specs = [pl.BlockSpec((tm, tk), lambda j, i, k: (i, k))]
    out_specs = [pl.BlockSpec((tm, tn), lambda j, i, k: (i, j))]
    out_shape = [jax.ShapeDtypeStruct((M, n_out), F32)]
    args = [x]
    if has_extra:
        me = extra.shape[0]
        in_specs.append(pl.BlockSpec((me, tk), lambda j, i, k: (0, k)))
        out_specs.append(pl.BlockSpec((me, tn), lambda j, i, k: (0, j)))
        out_shape.append(jax.ShapeDtypeStruct((me, n_out), F32))
        args.append(extra)
    in_specs.append(pl.BlockSpec((None,) * len(lead) + (tk, tn), lambda j, i, k: lead + (k, j)))
    args.append(w)
    outs = pl.pallas_call(
        body, grid=(n_out // tn, M // tm, nk),
        in_specs=in_specs, out_specs=out_specs, out_shape=out_shape,
        compiler_params=_params(3),
    )(*args)
    return tuple(outs) if has_extra else outs[0]


def _rms(v, g):
    return v * lax.rsqrt(jnp.mean(v * v, axis=-1, keepdims=True) + EPS) * g


def _prenorm_body(x_ref, g_ref, shift_ref, scale_ref, h_ref):
    y = _rms(x_ref[...], g_ref[...])
    h_ref[...] = (y * (1.0 + scale_ref[...]) + shift_ref[...]).astype(h_ref.dtype)


def _post_pre_body(*refs, res_w, n_next):
    x_ref, y_ref, gpost_ref, gate_ref = refs[:4]
    nxt = refs[4:4 + 3 * n_next]
    xo_ref = refs[4 + 3 * n_next]
    h_refs = refs[5 + 3 * n_next:]
    xn = x_ref[...] + (res_w * gate_ref[...]) * _rms(y_ref[...], gpost_ref[...])
    xo_ref[...] = xn
    for n in range(n_next):
        g_ref, shift_ref, scale_ref = nxt[3 * n:3 * n + 3]
        y = _rms(xn, g_ref[...])
        h_refs[n][...] = (y * (1.0 + scale_ref[...]) + shift_ref[...]).astype(h_refs[n].dtype)


def _seq_vec(vec, M, seq_len, tm):
    nseq, D = vec.shape
    if seq_len % tm == 0:
        per = seq_len // tm
        return vec.reshape(nseq, 1, D), pl.BlockSpec((None, 1, D), lambda i: (i // per, 0, 0))
    rows = jnp.broadcast_to(vec[:, None, :], (nseq, seq_len, D)).reshape(M, D)
    return rows, pl.BlockSpec((tm, D), lambda i: (i, 0))


def _gain(g):
    D = g.shape[-1]
    return g.reshape(1, D), pl.BlockSpec((1, D), lambda i: (0, 0))


def _prenorm(x, g, shift, scale, seq_len, tm=256):
    M, D = x.shape
    tm = _tile(M, tm, 16)
    row = pl.BlockSpec((tm, D), lambda i: (i, 0))
    ga, gs = _gain(g)
    sa, ss = _seq_vec(shift, M, seq_len, tm)
    ca, cs = _seq_vec(scale, M, seq_len, tm)
    return pl.pallas_call(
        _prenorm_body, grid=(M // tm,),
        in_specs=[row, gs, ss, cs], out_specs=row,
        out_shape=jax.ShapeDtypeStruct((M, D), BF16),
        compiler_params=_params(1),
    )(x, ga, sa, ca)


def _post_pre(x, y, g_post, gate, res_w, nexts, seq_len, tm=256):
    M, D = x.shape
    tm = _tile(M, tm, 16)
    row = pl.BlockSpec((tm, D), lambda i: (i, 0))
    ga, gs = _gain(g_post)
    ta, ts = _seq_vec(gate, M, seq_len, tm)
    args, specs = [x, y, ga, ta], [row, row, gs, ts]
    for g, shift, scale in nexts:
        a, s = _gain(g)
        args.append(a), specs.append(s)
        a, s = _seq_vec(shift, M, seq_len, tm)
        args.append(a), specs.append(s)
        a, s = _seq_vec(scale, M, seq_len, tm)
        args.append(a), specs.append(s)
    n_next = len(nexts)
    outs = pl.pallas_call(
        functools.partial(_post_pre_body, res_w=res_w, n_next=n_next),
        grid=(M // tm,), in_specs=specs,
        out_specs=[row] * (1 + n_next),
        out_shape=[jax.ShapeDtypeStruct((M, D), F32)] + [jax.ShapeDtypeStruct((M, D), BF16)] * n_next,
        compiler_params=_params(1),
    )(*args)
    return outs[0], list(outs[1:])


def _dwconv_body(prev_ref, u_ref, wdw_ref, bdw_ref, lng_ref, lnb_ref, d_ref, win_ref, sh_ref,
                 pre_ref, *, tt, width, rows_chunk, lanes_chunk):
    D = u_ref.shape[-1]
    i = pl.program_id(1)

    @pl.when(i == 0)
    def _():
        win_ref[0:CONV_HALO, :] = prev_ref[...]

    win_ref[CONV_HALO:CONV_HALO + tt, :] = u_ref[...]
    first = CONV_HALO - (width - 1)
    slab_rows = tt + CONV_HALO - SUBLANES

    for c in range(D // lanes_chunk):
        cs = slice(c * lanes_chunk, (c + 1) * lanes_chunk)
        for s in range(1, SUBLANES):
            sh_ref[s - 1, 0:slab_rows, :] = win_ref[s:s + slab_rows, cs]

        def row_body(r, carry, cs=cs):
            r0 = pl.multiple_of(r * rows_chunk, rows_chunk)
            acc = jnp.zeros((rows_chunk, lanes_chunk), F32)
            for w in range(width):
                a, s = divmod(first + w, SUBLANES)
                rows = pl.ds(r0 + a * SUBLANES, rows_chunk)
                tap = win_ref[rows, cs] if s == 0 else sh_ref[s - 1, rows, :]
                acc = acc + wdw_ref[w:w + 1, cs] * tap
            pre_ref[pl.ds(r0, rows_chunk), cs] = acc + bdw_ref[:, cs]
            return carry

        lax.fori_loop(0, tt // rows_chunk, row_body, 0)

    d = pre_ref[...]
    mu = jnp.mean(d, axis=-1, keepdims=True)
    var = jnp.mean(jnp.square(d - mu), axis=-1, keepdims=True)
    y = (d - mu) * lax.rsqrt(var + EPS) * lng_ref[...] + lnb_ref[...]
    d_ref[...] = (y * jax.nn.sigmoid(y)).astype(d_ref.dtype)

    @pl.when(i + 1 < pl.num_programs(1))
    def _():
        win_ref[0:CONV_HALO, :] = win_ref[tt:tt + CONV_HALO, :]


def _dwconv(u, prev, w_dw, b_dw, ln_g, ln_b, tt=256):
    B, T, D = u.shape
    width = w_dw.shape[0]
    assert width - 1 <= CONV_HALO
    tt = _tile(T, tt, SUBLANES)
    assert tt >= CONV_HALO or T == tt
    rows_chunk = _tile(tt, 32, SUBLANES)
    lanes_chunk = _tile(D, 512, LANES)
    vec = pl.BlockSpec((1, D), lambda b, i: (0, 0))
    body = functools.partial(_dwconv_body, tt=tt, width=width, rows_chunk=rows_chunk,
                             lanes_chunk=lanes_chunk)
    return pl.pallas_call(
        body, grid=(B, T // tt),
        in_specs=[pl.BlockSpec((None, CONV_HALO, D), lambda b, i: (b, 0, 0)),
                  pl.BlockSpec((None, tt, D), lambda b, i: (b, i, 0)),
                  pl.BlockSpec((width, D), lambda b, i: (0, 0)), vec, vec, vec],
        out_specs=pl.BlockSpec((None, tt, D), lambda b, i: (b, i, 0)),
        out_shape=jax.ShapeDtypeStruct((B, T, D), BF16),
        scratch_shapes=[pltpu.VMEM((CONV_HALO + tt, D), F32),
                        pltpu.VMEM((SUBLANES - 1, CONV_HALO + tt, lanes_chunk), F32),
                        pltpu.VMEM((tt, D), F32)],
        compiler_params=_params(2),
    )(prev, u, w_dw, b_dw.reshape(1, D), ln_g.reshape(1, D), ln_b.reshape(1, D))


def _log_sigmoid(z):
    return jnp.minimum(z, 0.0) - jnp.log(1.0 + jnp.exp(-jnp.abs(z)))


def _split_bf16(v):
    hi = v.astype(BF16)
    lo = (v - hi.astype(F32)).astype(BF16)
    return hi, lo


def _sb_prompt_body(bias_ref, q_ref, k_ref, v_ref, o_ref, *, tq, dh, heads, scale):
    hg = pl.program_id(1)
    qi = pl.program_id(2)
    row = lax.broadcasted_iota(jnp.int32, (tq, tq), 0)
    col = lax.broadcasted_iota(jnp.int32, (tq, tq), 1)
    visible = col < row
    later = jnp.where(row > col, 1.0, 0.0).astype(BF16)
    later2 = jnp.concatenate([later, later], axis=0)

    def scores(kblock, hh, diag):
        cs = slice(hh * dh, (hh + 1) * dh)
        kt = k_ref[pl.ds(pl.multiple_of(kblock * tq, tq), tq), cs].astype(BF16)
        z = lax.dot_general(q_ref[:, cs], kt, (((1,), (1,)), ((), ())),
                            preferred_element_type=F32) * scale + bias_ref[hg * heads + hh]
        lb = _log_sigmoid(z)
        l1m = lb - z
        if diag:
            l1m = jnp.where(visible, l1m, 0.0)
        hi, lo = _split_bf16(l1m)
        inner = jnp.dot(jnp.concatenate([hi, lo], axis=1), later2,
                        preferred_element_type=F32)
        return lb + inner, inner[:, 0:1] + l1m[:, 0:1]

    def attend(kblock, hh, sc, carry, acc, diag):
        cs = slice(hh * dh, (hh + 1) * dh)
        pre, total = sc
        vt = v_ref[pl.ds(pl.multiple_of(kblock * tq, tq), tq), cs].astype(BF16)
        att = jnp.exp(pre + carry)
        if diag:
            att = jnp.where(visible, att, 0.0)
        acc = acc + jnp.dot(att.astype(BF16), vt, preferred_element_type=F32)
        return total + carry, acc

    def store(accs):
        for hh in range(heads):
            o_ref[:, hh * dh:(hh + 1) * dh] = accs[hh].astype(o_ref.dtype)

    done, first = [], []
    for hh in range(heads):
        sc = scores(qi, hh, True)
        done.append(attend(qi, hh, sc, jnp.zeros((tq, 1), F32), jnp.zeros((tq, dh), F32), True))
        first.append(scores(jnp.maximum(qi - 1, 0), hh, False) + done[hh])
    first = tuple(first)

    @pl.when(qi == 0)
    def _():
        store([acc for _, acc in done])

    @pl.when(qi > 0)
    def _():

        def body(it, state):
            out = []
            for hh in range(heads):
                pre, total, carry, acc = state[hh]
                nxt = scores(qi - 1 - it, hh, False)
                out.append(nxt + attend(qi - it, hh, (pre, total), carry, acc, False))
            return tuple(out)

        state = lax.fori_loop(1, qi, body, first)
        accs = []
        for hh in range(heads):
            pre, total, carry, acc = state[hh]
            accs.append(attend(0, hh, (pre, total), carry, acc, False)[1])
        store(accs)


def _sb_prompt(q, k, v, bias, n_seq, seq_len, n_heads, tq=256, heads=4):
    M, D = q.shape
    dh = D // n_heads
    tq = _tile(seq_len, tq, LANES)
    nq = seq_len // tq
    heads = _tile(n_heads, heads, 1)
    wide = heads * dh
    body = functools.partial(_sb_prompt_body, tq=tq, dh=dh, heads=heads, scale=dh ** -0.5)
    kv_spec = pl.BlockSpec((seq_len, wide), lambda b, h, i: (b, h))
    return pl.pallas_call(
        body, grid=(n_seq, n_heads // heads, nq),
        in_specs=[pl.BlockSpec(memory_space=pltpu.SMEM),
                  pl.BlockSpec((tq, wide), lambda b, h, i: (b * nq + i, h)),
                  kv_spec, kv_spec],
        out_specs=pl.BlockSpec((tq, wide), lambda b, h, i: (b * nq + i, h)),
        out_shape=jax.ShapeDtypeStruct((M, D), BF16),
        compiler_params=_params(3),
    )(bias.astype(F32), q, k, v)


def _sb_sample_body(pt_ref, bias_ref, q_ref, kn_ref, vn_ref, *refs, pages_per_step, n_heads,
                    n_new, scale):
    k_refs = refs[:pages_per_step]
    v_refs = refs[pages_per_step:2 * pages_per_step]
    o_ref = refs[2 * pages_per_step]
    carry_ref, acc_ref = refs[2 * pages_per_step + 1:]
    s = pl.program_id(1)
    n_steps = pl.num_programs(1)
    q = q_ref[...]
    nq, dh = q.shape
    bias = bias_ref[...]
    head_of_row = lax.broadcasted_iota(jnp.int32, (1, n_heads, nq), 1)
    head_of_col = lax.broadcasted_iota(jnp.int32, (1, n_heads, nq), 2) % n_heads
    own = head_of_row == head_of_col

    def scores(k2, n_keys):
        zf = lax.dot_general(k2.astype(BF16), q, (((1,), (1,)), ((), ())),
                             preferred_element_type=F32)
        zc = jnp.sum(jnp.where(own, zf.reshape(n_keys, n_heads, nq), 0.0), axis=1)
        return zc * scale + bias

    def weighted(att, v2, n_keys):
        a3 = jnp.where(own, att[:, None, :], 0.0).astype(BF16).reshape(n_keys * n_heads, nq)
        return lax.dot_general(a3, v2.astype(BF16), (((0,), (0,)), ((), ())),
                               preferred_element_type=F32)

    @pl.when(s == 0)
    def _():
        n_keys = kn_ref.shape[0] // n_heads
        z = scores(kn_ref[...], n_keys)
        lb = _log_sigmoid(z)
        key = lax.broadcasted_iota(jnp.int32, (n_keys, nq), 0)
        t_of_col = lax.broadcasted_iota(jnp.int32, (n_keys, nq), 1) // n_heads
        mask = key < t_of_col
        l1m = jnp.where(mask, lb - z, 0.0)
        suffix = jnp.zeros((n_keys, nq), F32)
        for j in range(1, n_keys):
            suffix = suffix + jnp.where(key < j, l1m[j:j + 1, :], 0.0)
        att = jnp.where(mask, jnp.exp(lb + suffix), 0.0)
        acc_ref[...] = weighted(att, vn_ref[...], n_keys)
        carry_ref[...] = jnp.broadcast_to(jnp.sum(l1m, axis=0, keepdims=True), carry_ref.shape)

    for p in range(pages_per_step):
        kp = k_refs[p]
        n_keys = kp.shape[0]
        k2 = kp[...].reshape(n_keys * n_heads, dh)
        v2 = v_refs[p][...].reshape(n_keys * n_heads, dh)
        z = scores(k2, n_keys)
        lb = _log_sigmoid(z)
        l1m = lb - z
        hi, lo = _split_bf16(l1m)
        r = lax.broadcasted_iota(jnp.int32, (n_keys, n_keys), 0)
        c = lax.broadcasted_iota(jnp.int32, (n_keys, n_keys), 1)
        later = jnp.where(c > r, 1.0, 0.0).astype(BF16)
        carry = carry_ref[0:1, :]
        suffix = (jnp.dot(later, hi, preferred_element_type=F32)
                  + jnp.dot(later, lo, preferred_element_type=F32)) + carry
        att = jnp.exp(lb + suffix)
        acc_ref[...] += weighted(att, v2, n_keys)
        carry_ref[...] = jnp.broadcast_to(carry + jnp.sum(l1m, axis=0, keepdims=True),
                                          carry_ref.shape)

    @pl.when(s == n_steps - 1)
    def _():
        o_ref[...] = acc_ref[...].astype(o_ref.dtype)


def _sb_sample(q, k_new, v_new, cache_k, cache_v, page_table, bias, n_seq, n_new, n_heads,
               pages_per_step=4):
    M, D = q.shape
    dh = D // n_heads
    n_pages = page_table.shape[1]
    page = cache_k.shape[1]
    pages_per_step = _tile(n_pages, pages_per_step, 1)
    n_steps = n_pages // pages_per_step
    nq = n_new * n_heads
    pad_keys = SUBLANES
    q2 = q.reshape(n_seq, nq, dh)

    def pad_new(a):
        a = a.reshape(n_seq, nq, dh)
        return jnp.concatenate([a, jnp.zeros((n_seq, (pad_keys - n_new) * n_heads, dh), a.dtype)], axis=1)

    bias_row = jnp.tile(bias.astype(F32), n_new).reshape(1, nq)

    def page_spec(p):
        def imap(b, s, pt):
            logical = n_pages - 1 - (s * pages_per_step + p)
            return (pt[b * n_pages + logical], 0, 0, 0)
        return pl.BlockSpec((None, page, n_heads, dh), imap)

    seq_spec = lambda rows: pl.BlockSpec((None, rows, dh), lambda b, s, pt: (b, 0, 0))
    body = functools.partial(_sb_sample_body, pages_per_step=pages_per_step, n_heads=n_heads,
                             n_new=n_new, scale=dh ** -0.5)
    grid_spec = pltpu.PrefetchScalarGridSpec(
        num_scalar_prefetch=1, grid=(n_seq, n_steps),
        in_specs=[pl.BlockSpec((1, nq), lambda b, s, pt: (0, 0)),
                  seq_spec(nq), seq_spec(pad_keys * n_heads), seq_spec(pad_keys * n_heads)]
                 + [page_spec(p) for p in range(pages_per_step)] * 2,
        out_specs=seq_spec(nq),
        scratch_shapes=[pltpu.VMEM((SUBLANES, nq), F32), pltpu.VMEM((nq, dh), F32)])
    o = pl.pallas_call(
        body, grid_spec=grid_spec,
        out_shape=jax.ShapeDtypeStruct((n_seq, nq, dh), BF16),
        compiler_params=_params(2),
    )(page_table.reshape(-1).astype(jnp.int32), bias_row, q2, pad_new(k_new), pad_new(v_new),
      *([cache_k] * pages_per_step), *([cache_v] * pages_per_step))
    return o.reshape(M, D)


def _trunk(groups, weights):
    (norm_g, ffn_w_gate, ffn_w_up, ffn_w_down,
     conv_w_pw1, conv_b_pw1, conv_w_dw, conv_b_dw, conv_ln_g, conv_ln_b, conv_w_pw2, conv_b_pw2,
     kv_norm_g, w_kv, w_q, w_o, sb_bias) = weights
    big, small = groups
    D = big["x"].shape[1]
    depth = norm_g.shape[0]
    n_a = depth // 2
    n_heads = sb_bias.shape[1]
    d_ff = ffn_w_gate.shape[-1]

    def mod_of(g, l, sub, which):
        c0 = (sub * 3 + which) * D
        return g["mods"][l][:, c0:c0 + D]

    def kv_mod_of(g):
        return g["kv_mod"][:, :D], g["kv_mod"][:, D:]

    def mm(hs, ws, n_out, **kw):
        return list(_matmul(hs[0], ws, n_out, extra=hs[1], tk=D, **kw))

    def ffn(hs, l, s):
        acts = mm(hs, [(ffn_w_gate, (l, s), 0), (ffn_w_up, (l, s), 0)], d_ff,
                  combine=_swiglu_combine, out_dtype=BF16, tm=1024, tn=256)
        return list(_matmul_ktail(acts[0], ffn_w_down, (l, s), D, tm=1024, tn=2048, tk=DOWN_TK,
                                  extra=acts[1]))

    def post_pre(xs, ys, l, sub, res_w, nexts_of):
        res = [_post_pre(x, y, norm_g[l, sub, 1], mod_of(g, l, sub, 2), res_w, nexts_of(g),
                         g["seq_len"]) for g, x, y in zip(groups, xs, ys)]
        return [r[0] for r in res], [r[1] for r in res]

    def conv_mixer(g, u, l, states):
        n_seq, seq_len = g["n_seq"], g["seq_len"]
        u3 = u.reshape(n_seq, seq_len, D)
        prev = g["conv_prev"][l]
        keep = prev.shape[1]
        states.append(jnp.concatenate([prev, u3], axis=1)[:, -keep:])
        prev_pad = jnp.concatenate([jnp.zeros((n_seq, CONV_HALO - keep, D), F32), prev], axis=1)
        t_pad = -seq_len % SUBLANES
        u_in = u3 if t_pad == 0 else jnp.concatenate(
            [u3, jnp.zeros((n_seq, t_pad, D), F32)], axis=1)
        d = _dwconv(u_in, prev_pad, conv_w_dw[l], conv_b_dw[l], conv_ln_g[l], conv_ln_b[l])
        return d[:, :seq_len].reshape(n_seq * seq_len, D)

    def attention(g, q, k_new, v_new, j):
        if g["past"] is None:
            return _sb_prompt(q, k_new, v_new, sb_bias[j], g["n_seq"], g["seq_len"], n_heads)
        cache_k, cache_v, page_table = g["past"]
        return _sb_sample(q, k_new, v_new, cache_k, cache_v, page_table, sb_bias[j],
                          g["n_seq"], g["seq_len"], n_heads)

    xs = [g["x"] for g in groups]
    conv_states = [[] for _ in groups]
    k_news = v_news = None
    hs = [_prenorm(x, norm_g[0, 0, 0], mod_of(g, 0, 0, 0), mod_of(g, 0, 0, 1), g["seq_len"])
          for g, x in zip(groups, xs)]
    h_kvs = None
    for l in range(depth):
        if l == n_a:
            if h_kvs is None:
                h_kvs = [_prenorm(x, kv_norm_g, *kv_mod_of(g), g["seq_len"])
                         for g, x in zip(groups, xs)]
            k_news = mm(h_kvs, [(w_kv, (), 0)], D, tm=1024, tn=512)
            v_news = mm(h_kvs, [(w_kv, (), D)], D, tm=1024, tn=512)
        ys = ffn(hs, l, 0)
        xs, nxt = post_pre(xs, ys, l, 0, FFN_RES_W,
                           lambda g: [(norm_g[l, 1, 0], mod_of(g, l, 1, 0), mod_of(g, l, 1, 1))])
        hs = [n[0] for n in nxt]
        if l < n_a:
            us = mm(hs, [(conv_w_pw1, (l,), 0), (conv_w_pw1, (l,), D)], D,
                    biases=[conv_b_pw1[l, :D], conv_b_pw1[l, D:]], combine=_glu_combine,
                    tm=1024, tn=256)
            ds = [conv_mixer(g, u, l, st) for g, u, st in zip(groups, us, conv_states)]
            ys = mm(ds, [(conv_w_pw2, (l,), 0)], D, biases=[conv_b_pw2[l]], tm=1024, tn=512)
        else:
            j = l - n_a
            qs = mm(hs, [(w_q, (j,), 0)], D, out_dtype=BF16, tm=1024, tn=512)
            os_ = [attention(g, q, kn, vn, j) for g, q, kn, vn in zip(groups, qs, k_news, v_news)]
            ys = mm(os_, [(w_o, (j,), 0)], D, tm=1024, tn=512)
        xs, nxt = post_pre(xs, ys, l, 1, 1.0,
                           lambda g: [(norm_g[l, 2, 0], mod_of(g, l, 2, 0), mod_of(g, l, 2, 1))])
        hs = [n[0] for n in nxt]
        ys = ffn(hs, l, 1)

        def nexts_of(g):
            nexts = []
            if l + 1 < depth:
                nexts.append((norm_g[l + 1, 0, 0], mod_of(g, l + 1, 0, 0), mod_of(g, l + 1, 0, 1)))
                if l + 1 == n_a:
                    nexts.append((kv_norm_g,) + kv_mod_of(g))
            return nexts

        xs, nxt = post_pre(xs, ys, l, 2, FFN_RES_W, nexts_of)
        if nxt[0]:
            hs = [n[0] for n in nxt]
            h_kvs = [n[1] for n in nxt] if len(nxt[0]) > 1 else None
    return [(x, jnp.stack(st, axis=0), kn, vn)
            for x, st, kn, vn in zip(xs, conv_states, k_news, v_news)]


def kernel(x_prompt, x_sample, c_prompt, c_sample, state_conv, cache_k, cache_v, page_table, w_mod, b_mod, norm_g, ffn_w_gate, ffn_w_up, ffn_w_down, conv_w_pw1, conv_b_pw1, conv_w_dw, conv_b_dw, conv_ln_g, conv_ln_b, conv_w_pw2, conv_b_pw2, kv_w_mod, kv_b_mod, kv_norm_g, w_kv, w_q, w_o, sb_bias):
    B, T, D = x_prompt.shape
    Bd, Td, _ = x_sample.shape
    depth = w_mod.shape[0]
    n_heads = sb_bias.shape[1]
    dh = D // n_heads
    weights = (norm_g, ffn_w_gate, ffn_w_up, ffn_w_down,
               conv_w_pw1, conv_b_pw1, conv_w_dw, conv_b_dw, conv_ln_g, conv_ln_b, conv_w_pw2,
               conv_b_pw2, kv_norm_g, w_kv, w_q, w_o, sb_bias)

    n_c = B + Bd
    c_rows = -n_c % 16
    c_all = jnp.concatenate([c_prompt, c_sample, jnp.zeros((c_rows, D), F32)], axis=0)
    mods = [_matmul(c_all, [(w_mod, (l,), 0)], w_mod.shape[-1], biases=[b_mod[l]],
                    tm=16, tn=512, tk=D, lhs_silu=True) for l in range(depth)]
    kv_mod = _matmul(c_all, [(kv_w_mod, (), 0)], kv_w_mod.shape[-1], biases=[kv_b_mod],
                     tm=16, tn=512, tk=D, lhs_silu=True)

    conv_zero = jnp.zeros((depth // 2, B, conv_w_dw.shape[1] - 1, D), F32)
    prompt = dict(x=x_prompt.reshape(B * T, D), mods=[m[:B] for m in mods], kv_mod=kv_mod[:B],
                  conv_prev=conv_zero, past=None, n_seq=B, seq_len=T)
    sample = dict(x=x_sample.reshape(Bd * Td, D), mods=[m[B:n_c] for m in mods],
                  kv_mod=kv_mod[B:n_c], conv_prev=state_conv,
                  past=(cache_k, cache_v, page_table), n_seq=Bd, seq_len=Td)
    (y_p, cs_p, k_p, v_p), (y_s, cs_s, k_s, v_s) = _trunk([prompt, sample], weights)
    return (y_p.reshape(B, T, D), y_s.reshape(Bd, Td, D), cs_p, cs_s,
            k_p.reshape(B, T, n_heads, dh), v_p.reshape(B, T, n_heads, dh),
            k_s.reshape(Bd, Td, n_heads, dh), v_s.reshape(Bd, Td, n_heads, dh))
```
